```python
import math
import jax, jax.numpy as jnp
from jax import lax
import numpy as np

D_MODEL = 2048
BATCH = 8
SEQ = 4096
DEPTH = 1

CHUNK = 64
ATTN_WIDTH = D_MODEL // 2
SSM_WIDTH = D_MODEL - ATTN_WIDTH
HEAD_DIM = 64
N_HEADS = ATTN_WIDTH // HEAD_DIM
LEFT_CHUNKS = 8
BAND_CHUNKS = LEFT_CHUNKS + 1
REL_FUTURE = CHUNK - 1
REL_PAST = 128
N_REL = REL_FUTURE + REL_PAST + 1
NEG_INF = -1e30
SSM_GROUP = 16
N_SSM_GROUPS = SSM_WIDTH // SSM_GROUP
SSM_STATE = 64
DT_MIN = 1e-3
DT_MAX = 1e-1
D_FF = 256 * ((8 * D_MODEL // 3 + 255) // 256)
CONV_WIDTH = 3
LN_EPS = 1e-5
DEEPNORM_ALPHA = (2.0 * DEPTH) ** 0.25
DEEPNORM_BETA = (8.0 * DEPTH) ** -0.25

kernel_name = "hybrid_s5_chunkattn_deepnorm_encoder"


def layer_norm(x, g, b):
    xf = x.astype(jnp.float32)
    mu = jnp.mean(xf, axis=-1, keepdims=True)
    var = jnp.mean(jnp.square(xf - mu), axis=-1, keepdims=True)
    y = (xf - mu) * lax.rsqrt(var + LN_EPS) * g.astype(jnp.float32) + b.astype(jnp.float32)
    return y.astype(x.dtype)


def rms_norm(x, g):
    xf = x.astype(jnp.float32)
    ms = jnp.mean(jnp.square(xf), axis=-1, keepdims=True)
    return (xf * lax.rsqrt(ms + LN_EPS) * g.astype(jnp.float32)).astype(x.dtype)


def chunk_band_attention(q, k, v, rel_bias):
    bsz, seq, _ = q.shape
    nc = seq // CHUNK
    band = BAND_CHUNKS * CHUNK
    qc = q.reshape(bsz, nc, CHUNK, N_HEADS, HEAD_DIM)
    kc = k.reshape(bsz, nc, CHUNK, N_HEADS, HEAD_DIM)
    vc = v.reshape(bsz, nc, CHUNK, N_HEADS, HEAD_DIM)
    pad = ((0, 0), (LEFT_CHUNKS, 0), (0, 0), (0, 0), (0, 0))
    band_idx = np.arange(nc)[:, None] + np.arange(BAND_CHUNKS)[None, :]
    kb = jnp.pad(kc, pad)[:, band_idx].reshape(bsz, nc, band, N_HEADS, HEAD_DIM)
    vb = jnp.pad(vc, pad)[:, band_idx].reshape(bsz, nc, band, N_HEADS, HEAD_DIM)
    scores = jnp.einsum("bcqhd,bckhd->bhcqk", qc, kb).astype(jnp.float32) * (HEAD_DIM ** -0.5)
    rel = (LEFT_CHUNKS * CHUNK + np.arange(CHUNK)[:, None]) - np.arange(band)[None, :]
    rel_idx = np.clip(rel, -REL_FUTURE, REL_PAST) + REL_FUTURE
    bias = rel_bias.astype(jnp.float32)[:, rel_idx]
    valid = np.repeat(band_idx >= LEFT_CHUNKS, CHUNK, axis=1)
    scores = jnp.where(valid[None, None, :, None, :], scores + bias[None, :, None], NEG_INF)
    probs = jax.nn.softmax(scores, axis=-1).astype(v.dtype)
    out = jnp.einsum("bhcqk,bckhd->bcqhd", probs, vb)
    return out.reshape(bsz, seq, ATTN_WIDTH)


def _complex_linear_combine(e1, e2):
    a1r, a1i, b1r, b1i = e1
    a2r, a2i, b2r, b2i = e2
    return (a2r * a1r - a2i * a1i,
            a2r * a1i + a2i * a1r,
            a2r * b1r - a2i * b1i + b2r,
            a2r * b1i + a2i * b1r + b2i)


def s5_mixer(u, a_re, a_im, log_dt, b_re, b_im, c_re, c_im, d_skip):
    f32 = jnp.float32
    bsz, seq, _ = u.shape
    uf = u.astype(f32).reshape(bsz, seq, N_SSM_GROUPS, SSM_GROUP)
    a_re = a_re.astype(f32)
    a_im = a_im.astype(f32)
    dt = jnp.exp(log_dt.astype(f32))[:, None]
    decay = jnp.exp(dt * a_re)
    ab_re = decay * jnp.cos(dt * a_im)
    ab_im = decay * jnp.sin(dt * a_im)
    den = a_re * a_re + a_im * a_im
    zr = ab_re - 1.0
    f_re = (zr * a_re + ab_im * a_im) / den
    f_im = (ab_im * a_re - zr * a_im) / den
    b_re = b_re.astype(f32)
    b_im = b_im.astype(f32)
    bb_re = f_re[..., None] * b_re - f_im[..., None] * b_im
    bb_im = f_re[..., None] * b_im + f_im[..., None] * b_re
    bu_re = jnp.einsum("blgp,gnp->blgn", uf, bb_re)
    bu_im = jnp.einsum("blgp,gnp->blgn", uf, bb_im)
    shape = (1, seq, N_SSM_GROUPS, SSM_STATE)
    elems = (jnp.broadcast_to(ab_re, shape), jnp.broadcast_to(ab_im, shape), bu_re, bu_im)
    _, _, s_re, s_im = lax.associative_scan(_complex_linear_combine, elems, axis=1)
    y = (jnp.einsum("blgn,gpn->blgp", s_re, c_re.astype(f32))
         - jnp.einsum("blgn,gpn->blgp", s_im, c_im.astype(f32))
         + d_skip.astype(f32) * uf)
    return y.reshape(bsz, seq, SSM_WIDTH).astype(u.dtype)


def causal_depthwise_conv(u, w, b):
    out = lax.conv_general_dilated(
        u, w[:, None, :].astype(u.dtype), window_strides=(1,), padding=[(CONV_WIDTH - 1, 0)],
        dimension_numbers=("NWC", "WIO", "NWC"), feature_group_count=u.shape[-1])
    return out + b


def setup_inputs(seed: int = 0) -> dict:
    key = jax.random.key(seed)
    ks = jax.random.split(key, 24)
    f32 = jnp.float32
    L = DEPTH
    G, N, P = N_SSM_GROUPS, SSM_STATE, SSM_GROUP
    nrm = lambda k, s: jax.random.normal(k, s, f32)
    x = nrm(ks[0], (BATCH, SEQ, D_MODEL))
    w_in = nrm(ks[1], (L, D_MODEL, 3 * ATTN_WIDTH + SSM_WIDTH)) * D_MODEL ** -0.5
    attn_rel_bias = 0.1 * nrm(ks[2], (L, N_HEADS, N_REL))
    ssm_a_re = -0.5 + 0.01 * nrm(ks[3], (L, G, N))
    ssm_a_im = math.pi * jnp.arange(N, dtype=f32)[None, None, :] + 0.01 * nrm(ks[4], (L, G, N))
    ssm_log_dt = jax.random.uniform(ks[5], (L, G), f32, math.log(DT_MIN), math.log(DT_MAX))
    ssm_b_re = nrm(ks[6], (L, G, N, P)) * (2.0 * P) ** -0.5
    ssm_b_im = nrm(ks[7], (L, G, N, P)) * (2.0 * P) ** -0.5
    ssm_c_re = nrm(ks[8], (L, G, P, N)) * N ** -0.5
    ssm_c_im = nrm(ks[9], (L, G, P, N)) * N ** -0.5
    ssm_d = nrm(ks[10], (L, G, P))
    w_glu = nrm(ks[11], (L, SSM_WIDTH, SSM_WIDTH)) * SSM_WIDTH ** -0.5
    b_glu = 0.01 * nrm(ks[12], (L, SSM_WIDTH))
    g_attn_out = 1.0 + 0.01 * nrm(ks[13], (L, ATTN_WIDTH))
    g_ssm_out = 1.0 + 0.01 * nrm(ks[14], (L, SSM_WIDTH))
    w_out = nrm(ks[15], (L, D_MODEL, D_MODEL)) * D_MODEL ** -0.5 * DEEPNORM_BETA
    ln1_g = 1.0 + 0.01 * nrm(ks[16], (L, D_MODEL))
    ln1_b = 0.01 * nrm(ks[17], (L, D_MODEL))
    w_ffn_in = nrm(ks[18], (L, D_MODEL, 2 * D_FF)) * D_MODEL ** -0.5
    ffn_conv_w = nrm(ks[19], (L, CONV_WIDTH, D_FF)) * CONV_WIDTH ** -0.5
    ffn_conv_b = 0.01 * nrm(ks[20], (L, D_FF))
    w_ffn_out = nrm(ks[21], (L, D_FF, D_MODEL)) * D_FF ** -0.5 * DEEPNORM_BETA
    ln2_g = 1.0 + 0.01 * nrm(ks[22], (L, D_MODEL))
    ln2_b = 0.01 * nrm(ks[23], (L, D_MODEL))
    return {"x": x, "w_in": w_in, "attn_rel_bias": attn_rel_bias,
            "ssm_a_re": ssm_a_re, "ssm_a_im": ssm_a_im, "ssm_log_dt": ssm_log_dt,
            "ssm_b_re": ssm_b_re, "ssm_b_im": ssm_b_im, "ssm_c_re": ssm_c_re, "ssm_c_im": ssm_c_im,
            "ssm_d": ssm_d, "w_glu": w_glu, "b_glu": b_glu,
            "g_attn_out": g_attn_out, "g_ssm_out": g_ssm_out, "w_out": w_out,
            "ln1_g": ln1_g, "ln1_b": ln1_b, "w_ffn_in": w_ffn_in,
            "ffn_conv_w": ffn_conv_w, "ffn_conv_b": ffn_conv_b, "w_ffn_out": w_ffn_out,
            "ln2_g": ln2_g, "ln2_b": ln2_b}


def reference(x, w_in, attn_rel_bias, ssm_a_re, ssm_a_im, ssm_log_dt, ssm_b_re, ssm_b_im,
              ssm_c_re, ssm_c_im, ssm_d, w_glu, b_glu, g_attn_out, g_ssm_out, w_out,
              ln1_g, ln1_b, w_ffn_in, ffn_conv_w, ffn_conv_b, w_ffn_out, ln2_g, ln2_b):
    for l in range(DEPTH):
        proj = x @ w_in[l]
        q = proj[..., :ATTN_WIDTH]
        k = proj[..., ATTN_WIDTH:2 * ATTN_WIDTH]
        v = proj[..., 2 * ATTN_WIDTH:3 * ATTN_WIDTH]
        u = proj[..., 3 * ATTN_WIDTH:]
        attn = chunk_band_attention(q, k, v, attn_rel_bias[l])
        ssm = s5_mixer(u, ssm_a_re[l], ssm_a_im[l], ssm_log_dt[l], ssm_b_re[l], ssm_b_im[l],
                       ssm_c_re[l], ssm_c_im[l], ssm_d[l])
        ssm = jax.nn.gelu(ssm)
        ssm = ssm * jax.nn.sigmoid(ssm @ w_glu[l] + b_glu[l])
        mixed = jnp.concatenate([rms_norm(attn, g_attn_out[l]), rms_norm(ssm, g_ssm_out[l])], axis=-1)
        mixed = mixed @ w_out[l]
        x = layer_norm(DEEPNORM_ALPHA * x + mixed, ln1_g[l], ln1_b[l])
        up = x @ w_ffn_in[l]
        gate = causal_depthwise_conv(up[..., :D_FF], ffn_conv_w[l], ffn_conv_b[l])
        hidden = jax.nn.gelu(gate) * up[..., D_FF:]
        ff = hidden @ w_ffn_out[l]
        x = layer_norm(DEEPNORM_ALPHA * x + ff, ln2_g[l], ln2_b[l])
    return x
```

```python
import functools
import math

import jax
import jax.numpy as jnp
import numpy as np
from jax import lax
from jax.experimental import pallas as pl
from jax.experimental.pallas import tpu as pltpu

F32 = jnp.float32
BF16 = jnp.bfloat16

CHUNK = 64
HEAD_DIM = 64
LEFT_CHUNKS = 8
REL_FUTURE = CHUNK - 1
REL_PAST = 128
NEG_INF = -1e30
SSM_GROUP = 16
SSM_STATE = 64
CONV_WIDTH = 3
LN_EPS = 1e-5
DEPTH = 1
DEEPNORM_ALPHA = (2.0 * DEPTH) ** 0.25

LANES = 128
SUBLANES = 8
VMEM_LIMIT = 56 * 1024 * 1024

INPROJ_TM = 1024
INPROJ_TN = 1024
ATTN_BQ = 256
ATTN_PAD = LEFT_CHUNKS * CHUNK
SSM_TT = 32
SSM_PAIRS_PER_PASS = 8
OUT_TM = 512
FFN_TM = 512
FFN_TF = 512
FFN_HALO = 16


def _gelu_tanh(x):
    c = math.sqrt(2.0 / math.pi)
    return 0.5 * x * (1.0 + jnp.tanh(c * (x + 0.044715 * (x * x * x))))


def _layer_norm(h, g, b):
    mu = jnp.mean(h, axis=-1, keepdims=True)
    d = h - mu
    var = jnp.mean(d * d, axis=-1, keepdims=True)
    return d * lax.rsqrt(var + LN_EPS) * g + b


def _rms_norm(h, g):
    ms = jnp.mean(h * h, axis=-1, keepdims=True)
    return h * lax.rsqrt(ms + LN_EPS) * g


def _inproj_kernel(x_ref, w_ref, qkv_ref, u_ref, xb_ref, *, n_qkv_tiles):
    j = pl.program_id(1)

    @pl.when(j == 0)
    def _():
        xb_ref[...] = x_ref[...].astype(BF16)

    acc = jnp.dot(xb_ref[...], w_ref[...], preferred_element_type=F32)

    @pl.when(j < n_qkv_tiles)
    def _():
        qkv_ref[...] = acc.astype(BF16)

    @pl.when(j >= n_qkv_tiles)
    def _():
        u_ref[...] = acc


def _in_proj(x2, w_bf, n_qkv):
    t, d = x2.shape
    n = w_bf.shape[1]
    tm, tn = INPROJ_TM, INPROJ_TN
    n_u = n - n_qkv
    assert n_u == tn and n_qkv % tn == 0 and t % tm == 0
    nq_tiles = n_qkv // tn
    return pl.pallas_call(
        functools.partial(_inproj_kernel, n_qkv_tiles=nq_tiles),
        grid=(t // tm, n // tn),
        in_specs=[pl.BlockSpec((tm, d), lambda i, j: (i, 0)),
                  pl.BlockSpec((d, tn), lambda i, j: (0, j))],
        out_specs=[pl.BlockSpec((tm, tn), lambda i, j: (i, jnp.minimum(j, nq_tiles - 1))),
                   pl.BlockSpec((tm, tn), lambda i, j: (i, 0))],
        out_shape=[jax.ShapeDtypeStruct((t, n_qkv), BF16),
                   jax.ShapeDtypeStruct((t, n_u), F32)],
        scratch_shapes=[pltpu.VMEM((tm, d), BF16)],
        compiler_params=pltpu.CompilerParams(
            dimension_semantics=("arbitrary", "arbitrary"), vmem_limit_bytes=VMEM_LIMIT),
        name="in_proj",
    )(x2, w_bf)


def _attn_kernel(q_ref, k_ref, v_ref, tab_ref, o_ref, kp_ref, vp_ref, *, seq):
    bq = ATTN_BQ
    win = ATTN_PAD + bq
    nblk = seq // bq
    zeros = jnp.zeros((ATTN_PAD, LANES), BF16)
    kp_ref[0:ATTN_PAD, :] = zeros
    vp_ref[0:ATTN_PAD, :] = zeros
    kp_ref[ATTN_PAD:, :] = k_ref[...]
    vp_ref[ATTN_PAD:, :] = v_ref[...]
    lane = lax.broadcasted_iota(jnp.int32, (bq, LANES), 1)
    first_head = lane < HEAD_DIM
    col = lax.broadcasted_iota(jnp.int32, (1, win), 1)

    def body(i, carry):
        r0 = pl.multiple_of(i * bq, bq)
        q = q_ref[pl.ds(r0, bq), :]
        kw = kp_ref[pl.ds(r0, win), :]
        vw = vp_ref[pl.ds(r0, win), :]
        pen = jnp.where(col + (r0 - ATTN_PAD) < 0, NEG_INF, 0.0).astype(F32)
        outs = []
        for h in range(2):
            sel = first_head if h == 0 else jnp.logical_not(first_head)
            qh = jnp.where(sel, q, jnp.zeros_like(q))
            s = lax.dot_general(qh, kw, (((1,), (1,)), ((), ())), preferred_element_type=F32)
            s = s + tab_ref[h] + pen
            m = jnp.max(s, axis=-1, keepdims=True)
            p = jnp.exp(s - m)
            l = jnp.sum(p, axis=-1, keepdims=True)
            o = jnp.dot(p.astype(BF16), vw, preferred_element_type=F32)
            outs.append(o / l)
        o_ref[pl.ds(r0, bq), :] = jnp.where(first_head, outs[0], outs[1])
        return carry

    lax.fori_loop(0, nblk, body, 0)


def _attention(qkv3, table, n_heads):
    bsz, seq, width3 = qkv3.shape
    width = width3 // 3
    n_pairs = n_heads // 2
    assert width == n_heads * HEAD_DIM and 2 * HEAD_DIM == LANES and seq % ATTN_BQ == 0
    win = ATTN_PAD + ATTN_BQ
    blk = lambda off: pl.BlockSpec((None, seq, LANES), lambda b, p: (b, 0, off + p))
    return pl.pallas_call(
        functools.partial(_attn_kernel, seq=seq),
        grid=(bsz, n_pairs),
        in_specs=[blk(0), blk(n_pairs), blk(2 * n_pairs),
                  pl.BlockSpec((None, 2, ATTN_BQ, win), lambda b, p: (p, 0, 0, 0))],
        out_specs=pl.BlockSpec((None, seq, LANES), lambda b, p: (b, 0, p)),
        out_shape=jax.ShapeDtypeStruct((bsz, seq, width), F32),
        scratch_shapes=[pltpu.VMEM((ATTN_PAD + seq, LANES), BF16),
                        pltpu.VMEM((ATTN_PAD + seq, LANES), BF16)],
        compiler_params=pltpu.CompilerParams(
            dimension_semantics=("arbitrary", "arbitrary"), vmem_limit_bytes=VMEM_LIMIT),
        name="band_attention",
    )(qkv3, qkv3, qkv3, table)


def _bias_table(rel_bias):
    bq = ATTN_BQ
    win = ATTN_PAD + bq
    tq = ATTN_PAD + np.arange(bq)[:, None]
    tk = np.arange(win)[None, :]
    dchunk = tq // CHUNK - tk // CHUNK
    allowed = (dchunk >= 0) & (dchunk <= LEFT_CHUNKS)
    idx = np.clip(tq - tk, -REL_FUTURE, REL_PAST) + REL_FUTURE
    tab = jnp.where(allowed[None], rel_bias.astype(F32)[:, idx], NEG_INF)
    h = rel_bias.shape[0]
    return tab.reshape(h // 2, 2, bq, win)


def _ssm_kernel(u_ref, wb_ref, are_ref, aim_ref, wc_ref, d_ref, wglu_ref, bglu_ref, o_ref,
                st_re, st_im, bu_ref, s_ref, yg_ref, *, tt, n_pairs):
    ppp = SSM_PAIRS_PER_PASS
    pw = 2 * LANES
    slab_pairs = LANES // (2 * SSM_GROUP)

    @pl.when(pl.program_id(0) == 0)
    def _():
        st_re[...] = jnp.zeros_like(st_re)
        st_im[...] = jnp.zeros_like(st_im)

    ub = u_ref[...].astype(BF16)
    for pg in range(n_pairs // ppp):
        for q in range(ppp):
            gp = pg * ppp + q
            c0 = (gp // slab_pairs) * LANES
            bu_ref[:, q * pw:(q + 1) * pw] = jnp.dot(
                ub[:, c0:c0 + LANES], wb_ref[gp], preferred_element_type=F32)

        def step(t, carry):
            r0 = pl.multiple_of(t * SUBLANES, SUBLANES)
            new = []
            for q in range(ppp):
                gp = pg * ppp + q
                s_re, s_im = carry[2 * q], carry[2 * q + 1]
                a_re, a_im = are_ref[gp], aim_ref[gp]
                b_re = bu_ref[pl.ds(r0, SUBLANES), q * pw:q * pw + LANES]
                b_im = bu_ref[pl.ds(r0, SUBLANES), q * pw + LANES:(q + 1) * pw]
                n_re = a_re * s_re - a_im * s_im + b_re
                n_im = a_re * s_im + a_im * s_re + b_im
                s_ref[pl.ds(r0, SUBLANES), gp * pw:gp * pw + LANES] = n_re
                s_ref[pl.ds(r0, SUBLANES), gp * pw + LANES:(gp + 1) * pw] = n_im
                new += [n_re, n_im]
            return tuple(new)

        init = []
        for q in range(ppp):
            init += [st_re[pg * ppp + q], st_im[pg * ppp + q]]
        fin = lax.fori_loop(0, tt, step, tuple(init), unroll=2)
        for q in range(ppp):
            st_re[pg * ppp + q] = fin[2 * q]
            st_im[pg * ppp + q] = fin[2 * q + 1]

    n_out = wc_ref.shape[0]
    ow = wc_ref.shape[2]
    kw = wc_ref.shape[1]
    for j in range(n_out):
        y = jnp.dot(s_ref[:, j * kw:(j + 1) * kw].astype(BF16), wc_ref[j],
                    preferred_element_type=F32)
        y = y + d_ref[:, j * ow:(j + 1) * ow] * u_ref[:, j * ow:(j + 1) * ow]
        yg_ref[:, j * ow:(j + 1) * ow] = _gelu_tanh(y)
    yg = yg_ref[...]
    gate = jnp.dot(yg.astype(BF16), wglu_ref[...], preferred_element_type=F32) + bglu_ref[...]
    o_ref[...] = yg * (1.0 / (1.0 + jnp.exp(-gate)))


def _ssm(u_tm, wb, a_re, a_im, wc, d_row, wglu_bf, bglu_row, bsz):
    rows, width = u_tm.shape
    assert bsz == SUBLANES
    tt = SSM_TT
    r = tt * bsz
    n_pairs = wb.shape[0]
    const = lambda shape: pl.BlockSpec(shape, lambda i: (0,) * len(shape))
    return pl.pallas_call(
        functools.partial(_ssm_kernel, tt=tt, n_pairs=n_pairs),
        grid=(rows // r,),
        in_specs=[pl.BlockSpec((r, width), lambda i: (i, 0)),
                  const(wb.shape), const(a_re.shape), const(a_im.shape), const(wc.shape),
                  const(d_row.shape), const(wglu_bf.shape), const(bglu_row.shape)],
        out_specs=pl.BlockSpec((r, width), lambda i: (i, 0)),
        out_shape=jax.ShapeDtypeStruct((rows, width), F32),
        scratch_shapes=[pltpu.VMEM((n_pairs, SUBLANES, LANES), F32),
                        pltpu.VMEM((n_pairs, SUBLANES, LANES), F32),
                        pltpu.VMEM((r, SSM_PAIRS_PER_PASS * 2 * LANES), F32),
                        pltpu.VMEM((r, n_pairs * 2 * LANES), F32),
                        pltpu.VMEM((r, width), F32)],
        compiler_params=pltpu.CompilerParams(
            dimension_semantics=("arbitrary",), vmem_limit_bytes=VMEM_LIMIT),
        name="s5_glu",
    )(u_tm, wb, a_re, a_im, wc, d_row, wglu_bf, bglu_row)


def _ssm_params(a_re, a_im, log_dt, b_re, b_im, c_re, c_im):
    g, n = a_re.shape
    p = b_re.shape[2]
    a_re = a_re.astype(F32)
    a_im = a_im.astype(F32)
    dt = jnp.exp(log_dt.astype(F32))[:, None]
    decay = jnp.exp(dt * a_re)
    ab_re = decay * jnp.cos(dt * a_im)
    ab_im = decay * jnp.sin(dt * a_im)
    den = a_re * a_re + a_im * a_im
    zr = ab_re - 1.0
    f_re = (zr * a_re + ab_im * a_im) / den
    f_im = (ab_im * a_re - zr * a_im) / den
    b_re = b_re.astype(F32)
    b_im = b_im.astype(F32)
    bb_re = f_re[..., None] * b_re - f_im[..., None] * b_im
    bb_im = f_re[..., None] * b_im + f_im[..., None] * b_re
    eye = jnp.eye(g, dtype=F32)
    full_b = lambda bb: (eye[:, None, :, None] * jnp.swapaxes(bb, 1, 2)[:, :, None, :]).reshape(g * p, g * n)
    fb_re, fb_im = full_b(bb_re), full_b(bb_im)
    n_pairs = g // 2
    slab_pairs = LANES // (2 * p)
    wb = []
    for gp in range(n_pairs):
        c0 = (gp // slab_pairs) * LANES
        s0 = gp * 2 * n
        wb.append(jnp.concatenate([fb_re[c0:c0 + LANES, s0:s0 + 2 * n],
                                   fb_im[c0:c0 + LANES, s0:s0 + 2 * n]], axis=1))
    wb = jnp.stack(wb).astype(BF16)
    full_c = lambda cc: (eye[:, None, :, None] * jnp.swapaxes(cc.astype(F32), 1, 2)[:, :, None, :]).reshape(g * n, g * p)
    fc_re, fc_im = full_c(c_re), -full_c(c_im)
    ow = 2 * LANES
    pairs_per_out = ow // (2 * p)
    wc = []
    for j in range(g * p // ow):
        blocks = []
        for q in range(j * pairs_per_out, (j + 1) * pairs_per_out):
            s0 = q * 2 * n
            blocks += [fc_re[s0:s0 + 2 * n, j * ow:(j + 1) * ow], fc_im[s0:s0 + 2 * n, j * ow:(j + 1) * ow]]
        wc.append(jnp.concatenate(blocks, axis=0))
    wc = jnp.stack(wc).astype(BF16)
    bcast = lambda a: jnp.broadcast_to(a.reshape(n_pairs, 1, 2 * n), (n_pairs, SUBLANES, 2 * n))
    return wb, bcast(ab_re), bcast(ab_im), wc


def _outproj_kernel(attn_ref, ssm_ref, x_ref, wa_ref, ws_ref, ga_ref, gs_ref, g_ref, b_ref, o_ref):
    an = _rms_norm(attn_ref[...], ga_ref[...]).astype(BF16)
    sn = _rms_norm(ssm_ref[...], gs_ref[...]).astype(BF16)
    mixed = jnp.dot(an, wa_ref[...], preferred_element_type=F32)
    mixed = mixed + jnp.dot(sn, ws_ref[...], preferred_element_type=F32)
    h = DEEPNORM_ALPHA * x_ref[...] + mixed
    o_ref[...] = _layer_norm(h, g_ref[...], b_ref[...])


def _out_proj(attn, ssm, x2, wa, ws, ga, gs, g, b):
    t, d = x2.shape
    wdt = attn.shape[1]
    tm = OUT_TM
    row = lambda w: pl.BlockSpec((tm, w), lambda i: (i, 0))
    const = lambda shape: pl.BlockSpec(shape, lambda i: (0,) * len(shape))
    return pl.pallas_call(
        _outproj_kernel,
        grid=(t // tm,),
        in_specs=[row(wdt), row(wdt), row(d), const(wa.shape), const(ws.shape),
                  const(ga.shape), const(gs.shape), const(g.shape), const(b.shape)],
        out_specs=row(d),
        out_shape=jax.ShapeDtypeStruct((t, d), F32),
        compiler_params=pltpu.CompilerParams(
            dimension_semantics=("arbitrary",), vmem_limit_bytes=VMEM_LIMIT),
        name="out_proj_ln1",
    )(attn, ssm, x2, wa, ws, ga, gs, g, b)


def _ffn_kernel(x_ref, halo_ref, wg_ref, wv_ref, cw_ref, cb_ref, wo_ref, g_ref, b_ref, o_ref,
                xb_ref, p_ref, acc_ref, *, blocks_per_seq):
    i = pl.program_id(0)
    f = pl.program_id(1)
    tm = x_ref.shape[0]

    @pl.when(f == 0)
    def _():
        keep = (i % blocks_per_seq != 0).astype(F32)
        xb_ref[0:FFN_HALO, :] = (halo_ref[...] * keep).astype(BF16)
        xb_ref[FFN_HALO:, :] = x_ref[...].astype(BF16)
        acc_ref[...] = jnp.zeros_like(acc_ref)

    xb = xb_ref[...]
    p_ref[...] = jnp.dot(xb, wg_ref[...], preferred_element_type=F32)
    val = jnp.dot(xb[FFN_HALO:], wv_ref[...], preferred_element_type=F32)
    gate = cb_ref[...]
    for k in range(CONV_WIDTH):
        off = FFN_HALO - (CONV_WIDTH - 1) + k
        gate = gate + cw_ref[k:k + 1, :] * p_ref[off:off + tm, :]
    hidden = (_gelu_tanh(gate) * val).astype(BF16)
    acc_ref[...] += jnp.dot(hidden, wo_ref[...], preferred_element_type=F32)

    @pl.when(f == pl.num_programs(1) - 1)
    def _():
        h = DEEPNORM_ALPHA * x_ref[...] + acc_ref[...]
        o_ref[...] = _layer_norm(h, g_ref[...], b_ref[...])


def _ffn(x1, wg, wv, cw, cb, wo, g, b, seq):
    t, d = x1.shape
    dff = wg.shape[1]
    tm, tf = FFN_TM, FFN_TF
    assert seq % tm == 0 and dff % tf == 0 and tm % FFN_HALO == 0
    hb = tm // FFN_HALO
    const = lambda shape: pl.BlockSpec(shape, lambda i, f: (0,) * len(shape))
    return pl.pallas_call(
        functools.partial(_ffn_kernel, blocks_per_seq=seq // tm),
        grid=(t // tm, dff // tf),
        in_specs=[pl.BlockSpec((tm, d), lambda i, f: (i, 0)),
                  pl.BlockSpec((FFN_HALO, d), lambda i, f: (jnp.maximum(i * hb - 1, 0), 0)),
                  pl.BlockSpec((d, tf), lambda i, f: (0, f)),
                  pl.BlockSpec((d, tf), lambda i, f: (0, f)),
                  pl.BlockSpec((CONV_WIDTH, tf), lambda i, f: (0, f)),
                  pl.BlockSpec((1, tf), lambda i, f: (0, f)),
                  pl.BlockSpec((tf, d), lambda i, f: (f, 0)),
                  const(g.shape), const(b.shape)],
        out_specs=pl.BlockSpec((tm, d), lambda i, f: (i, 0)),
        out_shape=jax.ShapeDtypeStruct((t, d), F32),
        scratch_shapes=[pltpu.VMEM((tm + FFN_HALO, d), BF16),
                        pltpu.VMEM((tm + FFN_HALO, tf), F32),
                        pltpu.VMEM((tm, d), F32)],
        compiler_params=pltpu.CompilerParams(
            dimension_semantics=("arbitrary", "arbitrary"), vmem_limit_bytes=VMEM_LIMIT),
        name="ffn_ln2",
    )(x1, x1, wg, wv, cw, cb, wo, g, b)


def kernel(x, w_in, attn_rel_bias, ssm_a_re, ssm_a_im, ssm_log_dt, ssm_b_re, ssm_b_im, ssm_c_re,
           ssm_c_im, ssm_d, w_glu, b_glu, g_attn_out, g_ssm_out, w_out, ln1_g, ln1_b, w_ffn_in,
           ffn_conv_w, ffn_conv_b, w_ffn_out, ln2_g, ln2_b):
    bsz, seq, d = x.shape
    depth = w_in.shape[0]
    n_heads = attn_rel_bias.shape[1]
    attn_w = n_heads * HEAD_DIM
    ssm_w = d - attn_w
    dff = w_ffn_out.shape[1]
    row = lambda a: a.reshape(1, -1).astype(F32)
    x2 = x.reshape(bsz * seq, d)
    for l in range(depth):
        w_l = jnp.concatenate([w_in[l][:, :attn_w] * (HEAD_DIM ** -0.5), w_in[l][:, attn_w:]], axis=1)
        qkv, u = _in_proj(x2, w_l.astype(BF16), 3 * attn_w)
        attn = _attention(qkv.reshape(bsz, seq, 3 * attn_w), _bias_table(attn_rel_bias[l]), n_heads)
        wb, a_re, a_im, wc = _ssm_params(ssm_a_re[l], ssm_a_im[l], ssm_log_dt[l], ssm_b_re[l],
                                         ssm_b_im[l], ssm_c_re[l], ssm_c_im[l])
        u_tm = jnp.swapaxes(u.reshape(bsz, seq, ssm_w), 0, 1).reshape(seq * bsz, ssm_w)
        ssm_tm = _ssm(u_tm, wb, a_re, a_im, wc, row(ssm_d[l]), w_glu[l].astype(BF16), row(b_glu[l]), bsz)
        ssm = jnp.swapaxes(ssm_tm.reshape(seq, bsz, ssm_w), 0, 1).reshape(bsz * seq, ssm_w)
        wo = w_out[l].astype(BF16)
        x2 = _out_proj(attn.reshape(bsz * seq, attn_w), ssm, x2, wo[:attn_w], wo[attn_w:],
                       row(g_attn_out[l]), row(g_ssm_out[l]), row(ln1_g[l]), row(ln1_b[l]))
        wf = w_ffn_in[l].astype(BF16)
        x2 = _ffn(x2, wf[:, :dff], wf[:, dff:], ffn_conv_w[l].astype(F32), row(ffn_conv_b[l]),
                  w_ffn_out[l].astype(BF16), row(ln2_g[l]), row(ln2_b[l]), seq)
    return x2.reshape(bsz, seq, d)
```

```python
import functools
import math

import jax
import jax.numpy as jnp
import numpy as np
from jax import lax
from jax.experimental import pallas as pl
from jax.experimental.pallas import tpu as pltpu

F32 = jnp.float32
BF16 = jnp.bfloat16

CHUNK = 64
HEAD_DIM = 64
LEFT_CHUNKS = 8
REL_FUTURE = CHUNK - 1
REL_PAST = 128
NEG_INF = -1e30
SSM_GROUP = 16
SSM_STATE = 64
CONV_WIDTH = 3
LN_EPS = 1e-5
DEPTH = 1
DEEPNORM_ALPHA = (2.0 * DEPTH) ** 0.25
LOG2E = math.log2(math.e)

LANES = 128
SUBLANES = 8
VMEM_LIMIT = 56 * 1024 * 1024

INPROJ_TM = 1024
INPROJ_TN = 1024
ATTN_BQ = 256
ATTN_SUB = 32
ATTN_PAD = LEFT_CHUNKS * CHUNK
ATTN_PAD_BLOCKS = ATTN_PAD // ATTN_BQ
SSM_TT = 32
SSM_PAIRS_PER_PASS = 8
OUT_TM = 512
FFN_TM = 512
FFN_TF = 512
FFN_HALO = 16


def _gelu_tanh(x):
    c = math.sqrt(2.0 / math.pi)
    return 0.5 * x * (1.0 + jnp.tanh(c * (x + 0.044715 * (x * x * x))))


def _layer_norm(h, g, b):
    mu = jnp.mean(h, axis=-1, keepdims=True)
    d = h - mu
    var = jnp.mean(d * d, axis=-1, keepdims=True)
    return d * lax.rsqrt(var + LN_EPS) * g + b


def _rms_norm(h, g):
    ms = jnp.mean(h * h, axis=-1, keepdims=True)
    return h * lax.rsqrt(ms + LN_EPS) * g


def _inproj_kernel(x_ref, w_ref, qkv_ref, u_ref, xb_ref, *, n_qkv_tiles):
    j = pl.program_id(1)

    @pl.when(j == 0)
    def _():
        xb_ref[...] = x_ref[...].astype(BF16)

    acc = jnp.dot(xb_ref[...], w_ref[...], preferred_element_type=F32)

    @pl.when(j < n_qkv_tiles)
    def _():
        qkv_ref[...] = acc.astype(BF16)

    @pl.when(j >= n_qkv_tiles)
    def _():
        u_ref[...] = acc


def _in_proj(x2, w_bf, n_qkv):
    t, d = x2.shape
    n = w_bf.shape[1]
    tm, tn = INPROJ_TM, INPROJ_TN
    n_u = n - n_qkv
    assert n_u == tn and n_qkv % tn == 0 and t % tm == 0
    nq_tiles = n_qkv // tn
    return pl.pallas_call(
        functools.partial(_inproj_kernel, n_qkv_tiles=nq_tiles),
        grid=(t // tm, n // tn),
        in_specs=[pl.BlockSpec((tm, d), lambda i, j: (i, 0)),
                  pl.BlockSpec((d, tn), lambda i, j: (0, j))],
        out_specs=[pl.BlockSpec((tm, tn), lambda i, j: (i, jnp.minimum(j, nq_tiles - 1))),
                   pl.BlockSpec((tm, tn), lambda i, j: (i, 0))],
        out_shape=[jax.ShapeDtypeStruct((t, n_qkv), BF16),
                   jax.ShapeDtypeStruct((t, n_u), F32)],
        scratch_shapes=[pltpu.VMEM((tm, d), BF16)],
        compiler_params=pltpu.CompilerParams(
            dimension_semantics=("arbitrary", "arbitrary"), vmem_limit_bytes=VMEM_LIMIT),
        name="in_proj",
    )(x2, w_bf)


def _attn_kernel(q_ref, k_ref, v_ref, tab_ref, o_ref, kp_ref, vp_ref, s0_ref, s1_ref, p0_ref, p1_ref,
                 *, seq):
    bq = ATTN_BQ
    win = ATTN_PAD + bq
    nblk = seq // bq
    zeros = jnp.zeros((ATTN_PAD, LANES), BF16)
    kp_ref[0:ATTN_PAD, :] = zeros
    vp_ref[0:ATTN_PAD, :] = zeros
    kp_ref[ATTN_PAD:, :] = k_ref[...]
    vp_ref[ATTN_PAD:, :] = v_ref[...]
    lane = lax.broadcasted_iota(jnp.int32, (bq, LANES), 1)
    first_head = lane < HEAD_DIM
    s_refs = (s0_ref, s1_ref)
    p_refs = (p0_ref, p1_ref)

    def scores(i, h):
        r0 = pl.multiple_of(i * bq, bq)
        q = q_ref[pl.ds(r0, bq), :]
        sel = first_head if h == 0 else jnp.logical_not(first_head)
        qh = jnp.where(sel, q, jnp.zeros_like(q))
        kw = kp_ref[pl.ds(r0, win), :]
        s_refs[h][...] = lax.dot_general(qh, kw, (((1,), (1,)), ((), ())), preferred_element_type=F32)

    def finish(i, h):
        r0 = pl.multiple_of(i * bq, bq)
        tsel = jnp.minimum(i, ATTN_PAD_BLOCKS)
        sums = []
        for sub in range(bq // ATTN_SUB):
            rows = slice(sub * ATTN_SUB, (sub + 1) * ATTN_SUB)
            s = s_refs[h][rows, :] + tab_ref[tsel, h, rows, :]
            m = jnp.max(s, axis=-1, keepdims=True)
            p = jnp.exp2(s - m)
            sums.append(jnp.sum(p, axis=-1, keepdims=True))
            p_refs[h][rows, :] = p.astype(BF16)
        vw = vp_ref[pl.ds(r0, win), :]
        o = jnp.dot(p_refs[h][...], vw, preferred_element_type=F32)
        o = o / jnp.concatenate(sums, axis=0)
        if h == 0:
            o_ref[pl.ds(r0, bq), :] = o
        else:
            o_ref[pl.ds(r0, bq), :] = jnp.where(first_head, o_ref[pl.ds(r0, bq), :], o)

    scores(0, 0)

    def body(i, carry):
        scores(i, 1)
        finish(i, 0)
        scores(i + 1, 0)
        finish(i, 1)
        return carry

    lax.fori_loop(0, nblk - 1, body, 0)
    scores(nblk - 1, 1)
    finish(nblk - 1, 0)
    finish(nblk - 1, 1)


def _attention(qkv3, table, n_heads):
    bsz, seq, width3 = qkv3.shape
    width = width3 // 3
    n_pairs = n_heads // 2
    assert width == n_heads * HEAD_DIM and 2 * HEAD_DIM == LANES and seq % ATTN_BQ == 0
    assert ATTN_PAD % ATTN_BQ == 0 and seq // ATTN_BQ > ATTN_PAD_BLOCKS
    win = ATTN_PAD + ATTN_BQ
    blk = lambda off: pl.BlockSpec((None, seq, LANES), lambda b, p: (b, 0, off + p))
    return pl.pallas_call(
        functools.partial(_attn_kernel, seq=seq),
        grid=(bsz, n_pairs),
        in_specs=[blk(0), blk(n_pairs), blk(2 * n_pairs),
                  pl.BlockSpec((None,) + table.shape[1:], lambda b, p: (p, 0, 0, 0, 0))],
        out_specs=pl.BlockSpec((None, seq, LANES), lambda b, p: (b, 0, p)),
        out_shape=jax.ShapeDtypeStruct((bsz, seq, width), F32),
        scratch_shapes=[pltpu.VMEM((ATTN_PAD + seq, LANES), BF16),
                        pltpu.VMEM((ATTN_PAD + seq, LANES), BF16),
                        pltpu.VMEM((ATTN_BQ, win), F32),
                        pltpu.VMEM((ATTN_BQ, win), F32),
                        pltpu.VMEM((ATTN_BQ, win), BF16),
                        pltpu.VMEM((ATTN_BQ, win), BF16)],
        compiler_params=pltpu.CompilerParams(
            dimension_semantics=("arbitrary", "arbitrary"), vmem_limit_bytes=VMEM_LIMIT),
        name="band_attention",
    )(qkv3, qkv3, qkv3, table)


def _bias_table(rel_bias):
    bq = ATTN_BQ
    win = ATTN_PAD + bq
    h = rel_bias.shape[0]
    period = win + bq
    d = np.arange(period)
    d = np.where(d < win, d, d - period)
    idx = np.clip(ATTN_PAD - d, -REL_FUTURE, REL_PAST) + REL_FUTURE
    w = rel_bias.astype(F32)[:, idx] * LOG2E
    toe = jnp.tile(w, (1, bq))[:, :bq * (period - 1)].reshape(h, bq, period - 1)[:, :, :win]
    tq = ATTN_PAD + np.arange(bq)[:, None]
    tk = np.arange(win)[None, :]
    dchunk = tq // CHUNK - tk // CHUNK
    allowed = (dchunk >= 0) & (dchunk <= LEFT_CHUNKS)
    variants = []
    for v in range(ATTN_PAD_BLOCKS + 1):
        exists = tk + (v * bq - ATTN_PAD) >= 0 if v < ATTN_PAD_BLOCKS else np.ones_like(tk, bool)
        variants.append(jnp.where((allowed & exists)[None], toe, NEG_INF))
    tab = jnp.stack(variants, axis=1)
    nv = ATTN_PAD_BLOCKS + 1
    return tab.reshape(h // 2, 2, nv, bq, win).transpose(0, 2, 1, 3, 4)


def _ssm_kernel(u_ref, pin_ref, pout_ref, wb_ref, are_ref, aim_ref, wc_ref, d_ref, wglu_ref, bglu_ref,
                gs_ref, o_ref, st_re, st_im, utm_ref, bu0_ref, bu1_ref, s_ref, yg_ref, *, tt, n_pairs):
    ppp = SSM_PAIRS_PER_PASS
    pw = 2 * LANES
    slab_pairs = LANES // (2 * SSM_GROUP)
    n_pass = n_pairs // ppp
    rows = tt * SUBLANES
    width = u_ref.shape[-1]
    bu_refs = (bu0_ref, bu1_ref)

    @pl.when(pl.program_id(0) == 0)
    def _():
        st_re[...] = jnp.zeros_like(st_re)
        st_im[...] = jnp.zeros_like(st_im)

    u_bm = u_ref[...].reshape(rows, width)
    h1 = u_bm.astype(BF16)
    r1 = u_bm - h1.astype(F32)
    h2 = r1.astype(BF16)
    h3 = (r1 - h2.astype(F32)).astype(BF16)
    utm_ref[...] = jnp.dot(pin_ref[...], jnp.concatenate([h1, h2, h3], axis=0),
                           preferred_element_type=F32)

    def input_drive(pg):
        for q in range(ppp):
            gp = pg * ppp + q
            c0 = (gp // slab_pairs) * LANES
            bu_refs[pg % 2][:, q * pw:(q + 1) * pw] = jnp.dot(
                utm_ref[:, c0:c0 + LANES].astype(BF16), wb_ref[gp], preferred_element_type=F32)

    def recurrence(pg):
        bu_ref = bu_refs[pg % 2]
        state = [(st_re[pg * ppp + q], st_im[pg * ppp + q]) for q in range(ppp)]
        decay = [(are_ref[pg * ppp + q], aim_ref[pg * ppp + q]) for q in range(ppp)]
        held = [None] * ppp
        for t in range(tt):
            r0 = t * SUBLANES
            for q in range(ppp):
                gp = pg * ppp + q
                s_re, s_im = state[q]
                a_re, a_im = decay[q]
                b_re = bu_ref[r0:r0 + SUBLANES, q * pw:q * pw + LANES]
                b_im = bu_ref[r0:r0 + SUBLANES, q * pw + LANES:(q + 1) * pw]
                n_re = a_re * s_re - a_im * s_im + b_re
                n_im = a_re * s_im + a_im * s_re + b_im
                state[q] = (n_re, n_im)
                if t % 2 == 0:
                    held[q] = (n_re, n_im)
                else:
                    p0 = r0 - SUBLANES
                    s_ref[p0:p0 + 2 * SUBLANES, gp * pw:gp * pw + LANES] = jnp.concatenate(
                        [held[q][0], n_re], axis=0).astype(BF16)
                    s_ref[p0:p0 + 2 * SUBLANES, gp * pw + LANES:(gp + 1) * pw] = jnp.concatenate(
                        [held[q][1], n_im], axis=0).astype(BF16)
        for q in range(ppp):
            st_re[pg * ppp + q] = state[q][0]
            st_im[pg * ppp + q] = state[q][1]

    kw = wc_ref.shape[1]
    ow = wc_ref.shape[2]
    assert kw == ppp * pw and n_pass == wc_ref.shape[0]

    def readout(j):
        y = jnp.dot(s_ref[:, j * kw:(j + 1) * kw], wc_ref[j], preferred_element_type=F32)
        y = y + d_ref[:, j * ow:(j + 1) * ow] * utm_ref[:, j * ow:(j + 1) * ow]
        yg_ref[:, j * ow:(j + 1) * ow] = _gelu_tanh(y)

    input_drive(0)
    for pg in range(n_pass):
        if pg + 1 < n_pass:
            input_drive(pg + 1)
        recurrence(pg)
        readout(pg)

    yg = yg_ref[...]
    gate = jnp.dot(yg.astype(BF16), wglu_ref[...], preferred_element_type=F32) + bglu_ref[...]
    glu = yg * (1.0 / (1.0 + jnp.exp(-gate)))
    sn = _rms_norm(glu, gs_ref[...]).astype(BF16)
    out = jnp.dot(pout_ref[...], sn, preferred_element_type=F32).astype(BF16)
    o_ref[...] = out.reshape(o_ref.shape)


def _ssm(u3, wb, a_re, a_im, wc, d_row, wglu_bf, bglu_row, gs_row):
    bsz, seq, width = u3.shape
    assert bsz == SUBLANES
    tt = SSM_TT
    rows = tt * bsz
    n_pairs = wb.shape[0]
    r = np.arange(rows)
    perm = np.zeros((rows, rows), np.float32)
    perm[r, (r % bsz) * tt + r // bsz] = 1.0
    pin = jnp.asarray(np.concatenate([perm, perm, perm], axis=1), BF16)
    pout = jnp.asarray(perm.T, BF16)
    const = lambda shape: pl.BlockSpec(shape, lambda i: (0,) * len(shape))
    consts = (pin, pout, wb, a_re, a_im, wc, d_row, wglu_bf, bglu_row, gs_row)
    return pl.pallas_call(
        functools.partial(_ssm_kernel, tt=tt, n_pairs=n_pairs),
        grid=(seq // tt,),
        in_specs=[pl.BlockSpec((bsz, tt, width), lambda i: (0, i, 0))] + [const(c.shape) for c in consts],
        out_specs=pl.BlockSpec((bsz, tt, width), lambda i: (0, i, 0)),
        out_shape=jax.ShapeDtypeStruct((bsz, seq, width), BF16),
        scratch_shapes=[pltpu.VMEM((n_pairs, SUBLANES, LANES), F32),
                        pltpu.VMEM((n_pairs, SUBLANES, LANES), F32),
                        pltpu.VMEM((rows, width), F32),
                        pltpu.VMEM((rows, SSM_PAIRS_PER_PASS * 2 * LANES), F32),
                        pltpu.VMEM((rows, SSM_PAIRS_PER_PASS * 2 * LANES), F32),
                        pltpu.VMEM((rows, n_pairs * 2 * LANES), BF16),
                        pltpu.VMEM((rows, width), F32)],
        compiler_params=pltpu.CompilerParams(
            dimension_semantics=("arbitrary",), vmem_limit_bytes=VMEM_LIMIT),
        name="s5_glu",
    )(u3, *consts)


def _ssm_params(a_re, a_im, log_dt, b_re, b_im, c_re, c_im):
    g, n = a_re.shape
    p = b_re.shape[2]
    a_re = a_re.astype(F32)
    a_im = a_im.astype(F32)
    dt = jnp.exp(log_dt.astype(F32))[:, None]
    decay = jnp.exp(dt * a_re)
    ab_re = decay * jnp.cos(dt * a_im)
    ab_im = decay * jnp.sin(dt * a_im)
    den = a_re * a_re + a_im * a_im
    zr = ab_re - 1.0
    f_re = (zr * a_re + ab_im * a_im) / den
    f_im = (ab_im * a_re - zr * a_im) / den
    b_re = b_re.astype(F32)
    b_im = b_im.astype(F32)
    bb_re = f_re[..., None] * b_re - f_im[..., None] * b_im
    bb_im = f_re[..., None] * b_im + f_im[..., None] * b_re
    n_pairs = g // 2
    slab_pairs = LANES // (2 * p)
    eye2 = jnp.eye(2, dtype=F32)

    def pack_b(bb):
        bt = jnp.swapaxes(bb, 1, 2).reshape(n_pairs, 2, p, n)
        blk = jnp.einsum("qapn,ab->qapbn", bt, eye2).reshape(n_pairs, 2 * p, 2 * n)
        pos = jnp.asarray(np.eye(slab_pairs, dtype=np.float32)[np.arange(n_pairs) % slab_pairs])
        return jnp.einsum("qj,qrs->qjrs", pos, blk).reshape(n_pairs, LANES, 2 * n)

    wb = jnp.concatenate([pack_b(bb_re), pack_b(bb_im)], axis=2).astype(BF16)

    ow = 2 * LANES
    ppo = ow // (2 * p)
    n_out = g * p // ow
    cc = jnp.stack([c_re.astype(F32), -c_im.astype(F32)])
    ct = jnp.swapaxes(cc, 2, 3).reshape(2, n_out, ppo, 2, n, p)
    eye_q = jnp.eye(ppo, dtype=F32)
    wc = jnp.einsum("ajqgnp,qr,gh->jqagnrhp", ct, eye_q, eye2).reshape(n_out, ppo * 4 * n, ow).astype(BF16)
    bcast = lambda a: jnp.broadcast_to(a.reshape(n_pairs, 1, 2 * n), (n_pairs, SUBLANES, 2 * n))
    return wb, bcast(ab_re), bcast(ab_im), wc


def _outproj_kernel(attn_ref, sn_ref, x_ref, wa_ref, ws_ref, ga_ref, g_ref, b_ref, o_ref):
    an = _rms_norm(attn_ref[...], ga_ref[...]).astype(BF16)
    mixed = jnp.dot(an, wa_ref[...], preferred_element_type=F32)
    mixed = mixed + jnp.dot(sn_ref[...], ws_ref[...], preferred_element_type=F32)
    h = DEEPNORM_ALPHA * x_ref[...] + mixed
    o_ref[...] = _layer_norm(h, g_ref[...], b_ref[...])


def _out_proj(attn, sn, x2, wo, ga, g, b):
    t, d = x2.shape
    wa_rows = attn.shape[1]
    ws_rows = sn.shape[1]
    assert wa_rows == ws_rows and wa_rows + ws_rows == wo.shape[0]
    tm = OUT_TM
    row = lambda w: pl.BlockSpec((tm, w), lambda i: (i, 0))
    const = lambda shape: pl.BlockSpec(shape, lambda i: (0,) * len(shape))
    return pl.pallas_call(
        _outproj_kernel,
        grid=(t // tm,),
        in_specs=[row(wa_rows), row(ws_rows), row(d),
                  pl.BlockSpec((wa_rows, d), lambda i: (0, 0)),
                  pl.BlockSpec((ws_rows, d), lambda i: (1, 0)),
                  const(ga.shape), const(g.shape), const(b.shape)],
        out_specs=row(d),
        out_shape=jax.ShapeDtypeStruct((t, d), F32),
        compiler_params=pltpu.CompilerParams(
            dimension_semantics=("arbitrary",), vmem_limit_bytes=VMEM_LIMIT),
        name="out_proj_ln1",
    )(attn, sn, x2, wo, wo, ga, g, b)


def _ffn_kernel(x_ref, halo_ref, wg_ref, wv_ref, cw_ref, cb_ref, wo_ref, g_ref, b_ref, o_ref,
                xb_ref, p_ref, acc_ref, *, blocks_per_seq):
    i = pl.program_id(0)
    f = pl.program_id(1)
    tm = x_ref.shape[0]

    @pl.when(f == 0)
    def _():
        keep = (i % blocks_per_seq != 0).astype(F32)
        xb_ref[0:FFN_HALO, :] = (halo_ref[...] * keep).astype(BF16)
        xb_ref[FFN_HALO:, :] = x_ref[...].astype(BF16)
        acc_ref[...] = jnp.zeros_like(acc_ref)

    xb = xb_ref[...]
    p_ref[...] = jnp.dot(xb, wg_ref[...], preferred_element_type=F32)
    val = jnp.dot(xb[FFN_HALO:], wv_ref[...], preferred_element_type=F32)
    gate = cb_ref[...]
    for k in range(CONV_WIDTH):
        off = FFN_HALO - (CONV_WIDTH - 1) + k
        gate = gate + cw_ref[k:k + 1, :] * p_ref[off:off + tm, :]
    hidden = (_gelu_tanh(gate) * val).astype(BF16)
    acc_ref[...] += jnp.dot(hidden, wo_ref[...], preferred_element_type=F32)

    @pl.when(f == pl.num_programs(1) - 1)
    def _():
        h = DEEPNORM_ALPHA * x_ref[...] + acc_ref[...]
        o_ref[...] = _layer_norm(h, g_ref[...], b_ref[...])


def _ffn(x1, w_in_bf, cw, cb, wo, g, b, seq):
    t, d = x1.shape
    dff = wo.shape[0]
    tm, tf = FFN_TM, FFN_TF
    assert seq % tm == 0 and dff % tf == 0 and tm % FFN_HALO == 0 and w_in_bf.shape[1] == 2 * dff
    hb = tm // FFN_HALO
    nf = dff // tf
    const = lambda shape: pl.BlockSpec(shape, lambda i, f: (0,) * len(shape))
    return pl.pallas_call(
        functools.partial(_ffn_kernel, blocks_per_seq=seq // tm),
        grid=(t // tm, nf),
        in_specs=[pl.BlockSpec((tm, d), lambda i, f: (i, 0)),
                  pl.BlockSpec((FFN_HALO, d), lambda i, f: (jnp.maximum(i * hb - 1, 0), 0)),
                  pl.BlockSpec((d, tf), lambda i, f: (0, f)),
                  pl.BlockSpec((d, tf), lambda i, f: (0, f + nf)),
                  pl.BlockSpec((CONV_WIDTH, tf), lambda i, f: (0, f)),
                  pl.BlockSpec((1, tf), lambda i, f: (0, f)),
                  pl.BlockSpec((tf, d), lambda i, f: (f, 0)),
                  const(g.shape), const(b.shape)],
        out_specs=pl.BlockSpec((tm, d), lambda i, f: (i, 0)),
        out_shape=jax.ShapeDtypeStruct((t, d), F32),
        scratch_shapes=[pltpu.VMEM((tm + FFN_HALO, d), BF16),
                        pltpu.VMEM((tm + FFN_HALO, tf), F32),
                        pltpu.VMEM((tm, d), F32)],
        compiler_params=pltpu.CompilerParams(
            dimension_semantics=("arbitrary", "arbitrary"), vmem_limit_bytes=VMEM_LIMIT),
        name="ffn_ln2",
    )(x1, x1, w_in_bf, w_in_bf, cw, cb, wo, g, b)


def kernel(x, w_in, attn_rel_bias, ssm_a_re, ssm_a_im, ssm_log_dt, ssm_b_re, ssm_b_im, ssm_c_re,
           ssm_c_im, ssm_d, w_glu, b_glu, g_attn_out, g_ssm_out, w_out, ln1_g, ln1_b, w_ffn_in,
           ffn_conv_w, ffn_conv_b, w_ffn_out, ln2_g, ln2_b):
    bsz, seq, d = x.shape
    depth = w_in.shape[0]
    n_heads = attn_rel_bias.shape[1]
    attn_w = n_heads * HEAD_DIM
    ssm_w = d - attn_w
    row = lambda a: a.reshape(1, -1).astype(F32)
    x2 = x.reshape(bsz * seq, d)
    for l in range(depth):
        col_scale = jnp.where(jnp.arange(w_in.shape[2]) < attn_w, HEAD_DIM ** -0.5 * LOG2E, 1.0)
        qkv, u = _in_proj(x2, (w_in[l] * col_scale).astype(BF16), 3 * attn_w)
        attn = _attention(qkv.reshape(bsz, seq, 3 * attn_w), _bias_table(attn_rel_bias[l]), n_heads)
        wb, a_re, a_im, wc = _ssm_params(ssm_a_re[l], ssm_a_im[l], ssm_log_dt[l], ssm_b_re[l],
                                         ssm_b_im[l], ssm_c_re[l], ssm_c_im[l])
        sn = _ssm(u.reshape(bsz, seq, ssm_w), wb, a_re, a_im, wc, row(ssm_d[l]),
                  w_glu[l].astype(BF16), row(b_glu[l]), row(g_ssm_out[l]))
        x2 = _out_proj(attn.reshape(bsz * seq, attn_w), sn.reshape(bsz * seq, ssm_w), x2,
                       w_out[l].astype(BF16), row(g_attn_out[l]), row(ln1_g[l]), row(ln1_b[l]))
        x2 = _ffn(x2, w_ffn_in[l].astype(BF16), ffn_conv_w[l].astype(F32), row(ffn_conv_b[l]),
                  w_ffn_out[l].astype(BF16), row(ln2_g[l]), row(ln2_b[l]), seq)
    return x2.reshape(bsz, seq, d)
```

```python
import functools
import math

import jax
import jax.numpy as jnp
import numpy as np
from jax import lax
from jax.experimental import pallas as pl
from jax.experimental.pallas import tpu as pltpu

F32 = jnp.float32
BF16 = jnp.bfloat16

CHUNK = 64
HEAD_DIM = 64
LEFT_CHUNKS = 8
REL_FUTURE = CHUNK - 1
REL_PAST = 128
NEG_INF = -1e30
SSM_GROUP = 16
SSM_STATE = 64
CONV_WIDTH = 3
LN_EPS = 1e-5
DEPTH = 1
DEEPNORM_ALPHA = (2.0 * DEPTH) ** 0.25
LOG2E = math.log2(math.e)

LANES = 128
SUBLANES = 8
VMEM_LIMIT = 56 * 1024 * 1024
FFN_VMEM_LIMIT = 60 * 1024 * 1024

INPROJ_TM = 1024
INPROJ_TN = 1024
ATTN_BQ = 256
ATTN_SUB = 32
ATTN_PAD = LEFT_CHUNKS * CHUNK
ATTN_PAD_BLOCKS = ATTN_PAD // ATTN_BQ
SSM_TT = 32
SSM_PAIRS_PER_PASS = 8
OUT_TM = 512
OUT_SUB = 256
FFN_TM = 1024
FFN_TF = 512
FFN_HALO = 16


def _gelu_tanh(x):
    c = math.sqrt(2.0 / math.pi)
    return 0.5 * x * (1.0 + jnp.tanh(c * (x + 0.044715 * (x * x * x))))


def _layer_norm(h, g, b):
    mu = jnp.mean(h, axis=-1, keepdims=True)
    d = h - mu
    var = jnp.mean(d * d, axis=-1, keepdims=True)
    return d * lax.rsqrt(var + LN_EPS) * g + b


def _rms_norm(h, g):
    ms = jnp.mean(h * h, axis=-1, keepdims=True)
    return h * lax.rsqrt(ms + LN_EPS) * g


def _inproj_kernel(x_ref, w_ref, qkv_ref, u_ref, xb_ref, *, n_qkv_tiles):
    j = pl.program_id(1)

    @pl.when(j == 0)
    def _():
        xb_ref[...] = x_ref[...].astype(BF16)

    acc = jnp.dot(xb_ref[...], w_ref[...], preferred_element_type=F32)

    @pl.when(j < n_qkv_tiles)
    def _():
        qkv_ref[...] = acc.astype(BF16)

    @pl.when(j >= n_qkv_tiles)
    def _():
        u_ref[...] = acc


def _in_proj(x2, w_bf, n_qkv):
    t, d = x2.shape
    n = w_bf.shape[1]
    tm, tn = INPROJ_TM, INPROJ_TN
    n_u = n - n_qkv
    assert n_u == tn and n_qkv % tn == 0 and t % tm == 0
    nq_tiles = n_qkv // tn
    return pl.pallas_call(
        functools.partial(_inproj_kernel, n_qkv_tiles=nq_tiles),
        grid=(t // tm, n // tn),
        in_specs=[pl.BlockSpec((tm, d), lambda i, j: (i, 0)),
                  pl.BlockSpec((d, tn), lambda i, j: (0, j))],
        out_specs=[pl.BlockSpec((tm, tn), lambda i, j: (i, jnp.minimum(j, nq_tiles - 1))),
                   pl.BlockSpec((tm, tn), lambda i, j: (i, 0))],
        out_shape=[jax.ShapeDtypeStruct((t, n_qkv), BF16),
                   jax.ShapeDtypeStruct((t, n_u), F32)],
        scratch_shapes=[pltpu.VMEM((tm, d), BF16)],
        compiler_params=pltpu.CompilerParams(
            dimension_semantics=("arbitrary", "arbitrary"), vmem_limit_bytes=VMEM_LIMIT),
        name="in_proj",
    )(x2, w_bf)


def _attn_kernel(q_ref, k_ref, v_ref, gen_ref, o_ref, kp_ref, vp_ref, s0_ref, s1_ref, p0_ref, p1_ref,
                 tab_ref, *, seq):
    bq = ATTN_BQ
    win = ATTN_PAD + bq
    nblk = seq // bq
    zeros = jnp.zeros((ATTN_PAD, LANES), BF16)
    kp_ref[0:ATTN_PAD, :] = zeros
    vp_ref[0:ATTN_PAD, :] = zeros
    kp_ref[ATTN_PAD:, :] = k_ref[...]
    vp_ref[ATTN_PAD:, :] = v_ref[...]
    lane = lax.broadcasted_iota(jnp.int32, (bq, LANES), 1)
    first_head = lane < HEAD_DIM
    s_refs = (s0_ref, s1_ref)
    p_refs = (p0_ref, p1_ref)

    @pl.when(pl.program_id(1) == 0)
    def _():
        for v in range(ATTN_PAD_BLOCKS + 1):
            tab_ref[v] = gen_ref[...]
        for v in range(ATTN_PAD_BLOCKS):
            front = ATTN_PAD - v * bq
            tab_ref[v, :, :, 0:front] = jnp.full((2, bq, front), NEG_INF, F32)

    def scores(i, h):
        r0 = pl.multiple_of(i * bq, bq)
        q = q_ref[pl.ds(r0, bq), :]
        sel = first_head if h == 0 else jnp.logical_not(first_head)
        qh = jnp.where(sel, q, jnp.zeros_like(q))
        kw = kp_ref[pl.ds(r0, win), :]
        s_refs[h][...] = lax.dot_general(qh, kw, (((1,), (1,)), ((), ())), preferred_element_type=F32)

    def finish(i, h):
        r0 = pl.multiple_of(i * bq, bq)
        tsel = jnp.minimum(i, ATTN_PAD_BLOCKS)
        sums = []
        for sub in range(bq // ATTN_SUB):
            rows = slice(sub * ATTN_SUB, (sub + 1) * ATTN_SUB)
            s = s_refs[h][rows, :] + tab_ref[tsel, h, rows, :]
            m = jnp.max(s, axis=-1, keepdims=True)
            p = jnp.exp2(s - m)
            sums.append(jnp.sum(p, axis=-1, keepdims=True))
            p_refs[h][rows, :] = p.astype(BF16)
        vw = vp_ref[pl.ds(r0, win), :]
        o = jnp.dot(p_refs[h][...], vw, preferred_element_type=F32)
        o = o / jnp.concatenate(sums, axis=0)
        if h == 0:
            o_ref[pl.ds(r0, bq), :] = o
        else:
            o_ref[pl.ds(r0, bq), :] = jnp.where(first_head, o_ref[pl.ds(r0, bq), :], o)

    scores(0, 0)

    def body(i, carry):
        scores(i, 1)
        finish(i, 0)
        scores(i + 1, 0)
        finish(i, 1)
        return carry

    lax.fori_loop(0, nblk - 1, body, 0)
    scores(nblk - 1, 1)
    finish(nblk - 1, 0)
    finish(nblk - 1, 1)


def _attention(qkv3, table, n_heads):
    bsz, seq, width3 = qkv3.shape
    width = width3 // 3
    n_pairs = n_heads // 2
    assert width == n_heads * HEAD_DIM and 2 * HEAD_DIM == LANES and seq % ATTN_BQ == 0
    assert ATTN_PAD % ATTN_BQ == 0 and seq // ATTN_BQ > ATTN_PAD_BLOCKS
    win = ATTN_PAD + ATTN_BQ
    assert table.shape == (n_pairs, 2, ATTN_BQ, win)
    blk = lambda off: pl.BlockSpec((None, seq, LANES), lambda p, b: (b, 0, off + p))
    return pl.pallas_call(
        functools.partial(_attn_kernel, seq=seq),
        grid=(n_pairs, bsz),
        in_specs=[blk(0), blk(n_pairs), blk(2 * n_pairs),
                  pl.BlockSpec((None, 2, ATTN_BQ, win), lambda p, b: (p, 0, 0, 0))],
        out_specs=pl.BlockSpec((None, seq, LANES), lambda p, b: (b, 0, p)),
        out_shape=jax.ShapeDtypeStruct((bsz, seq, width), F32),
        scratch_shapes=[pltpu.VMEM((ATTN_PAD + seq, LANES), BF16),
                        pltpu.VMEM((ATTN_PAD + seq, LANES), BF16),
                        pltpu.VMEM((ATTN_BQ, win), F32),
                        pltpu.VMEM((ATTN_BQ, win), F32),
                        pltpu.VMEM((ATTN_BQ, win), BF16),
                        pltpu.VMEM((ATTN_BQ, win), BF16),
                        pltpu.VMEM((ATTN_PAD_BLOCKS + 1, 2, ATTN_BQ, win), F32)],
        compiler_params=pltpu.CompilerParams(
            dimension_semantics=("arbitrary", "arbitrary"), vmem_limit_bytes=VMEM_LIMIT),
        name="band_attention",
    )(qkv3, qkv3, qkv3, table)


def _bias_table(rel_bias):
    bq = ATTN_BQ
    win = ATTN_PAD + bq
    h = rel_bias.shape[0]
    period = win + bq
    d = np.arange(period)
    d = np.where(d < win, d, d - period)
    idx = np.clip(ATTN_PAD - d, -REL_FUTURE, REL_PAST) + REL_FUTURE
    w = rel_bias.astype(F32)[:, idx] * LOG2E
    toe = jnp.tile(w, (1, bq))[:, :bq * (period - 1)].reshape(h, bq, period - 1)[:, :, :win]
    tq = ATTN_PAD + np.arange(bq)[:, None]
    tk = np.arange(win)[None, :]
    dchunk = tq // CHUNK - tk // CHUNK
    allowed = (dchunk >= 0) & (dchunk <= LEFT_CHUNKS)
    return jnp.where(allowed[None], toe, NEG_INF).reshape(h // 2, 2, bq, win)


def _ssm_kernel(u_ref, pin_ref, pout_ref, wb_ref, are_ref, aim_ref, wc_ref, d_ref, wglu_ref, bglu_ref,
                gs_ref, o_ref, st_re, st_im, utm_ref, bu0_ref, bu1_ref, s_ref, yg_ref, *, tt, n_pairs):
    ppp = SSM_PAIRS_PER_PASS
    pw = 2 * LANES
    slab_pairs = LANES // (2 * SSM_GROUP)
    n_pass = n_pairs // ppp
    rows = tt * SUBLANES
    width = u_ref.shape[-1]
    bu_refs = (bu0_ref, bu1_ref)

    @pl.when(pl.program_id(0) == 0)
    def _():
        st_re[...] = jnp.zeros_like(st_re)
        st_im[...] = jnp.zeros_like(st_im)

    u_bm = u_ref[...].reshape(rows, width)
    h1 = u_bm.astype(BF16)
    r1 = u_bm - h1.astype(F32)
    h2 = r1.astype(BF16)
    h3 = (r1 - h2.astype(F32)).astype(BF16)
    utm_ref[...] = jnp.dot(pin_ref[...], jnp.concatenate([h1, h2, h3], axis=0),
                           preferred_element_type=F32)

    def input_drive(pg):
        for q in range(ppp):
            gp = pg * ppp + q
            c0 = (gp // slab_pairs) * LANES
            bu_refs[pg % 2][:, q * pw:(q + 1) * pw] = jnp.dot(
                utm_ref[:, c0:c0 + LANES].astype(BF16), wb_ref[gp], preferred_element_type=F32)

    def recurrence(pg):
        bu_ref = bu_refs[pg % 2]
        state = [(st_re[pg * ppp + q], st_im[pg * ppp + q]) for q in range(ppp)]
        decay = [(are_ref[pg * ppp + q], aim_ref[pg * ppp + q]) for q in range(ppp)]
        held = [None] * ppp
        for t in range(tt):
            r0 = t * SUBLANES
            for q in range(ppp):
                gp = pg * ppp + q
                s_re, s_im = state[q]
                a_re, a_im = decay[q]
                b_re = bu_ref[r0:r0 + SUBLANES, q * pw:q * pw + LANES]
                b_im = bu_ref[r0:r0 + SUBLANES, q * pw + LANES:(q + 1) * pw]
                n_re = a_re * s_re - a_im * s_im + b_re
                n_im = a_re * s_im + a_im * s_re + b_im
                state[q] = (n_re, n_im)
                if t % 2 == 0:
                    held[q] = (n_re, n_im)
                else:
                    p0 = r0 - SUBLANES
                    s_ref[p0:p0 + 2 * SUBLANES, gp * pw:gp * pw + LANES] = jnp.concatenate(
                        [held[q][0], n_re], axis=0).astype(BF16)
                    s_ref[p0:p0 + 2 * SUBLANES, gp * pw + LANES:(gp + 1) * pw] = jnp.concatenate(
                        [held[q][1], n_im], axis=0).astype(BF16)
        for q in range(ppp):
            st_re[pg * ppp + q] = state[q][0]
            st_im[pg * ppp + q] = state[q][1]

    kw = wc_ref.shape[1]
    ow = wc_ref.shape[2]
    assert kw == ppp * pw and n_pass == wc_ref.shape[0]

    def readout(j):
        y = jnp.dot(s_ref[:, j * kw:(j + 1) * kw], wc_ref[j], preferred_element_type=F32)
        y = y + d_ref[:, j * ow:(j + 1) * ow] * utm_ref[:, j * ow:(j + 1) * ow]
        yg_ref[:, j * ow:(j + 1) * ow] = _gelu_tanh(y)

    input_drive(0)
    for pg in range(n_pass):
        if pg + 1 < n_pass:
            input_drive(pg + 1)
        recurrence(pg)
        readout(pg)

    yg = yg_ref[...]
    gate = jnp.dot(yg.astype(BF16), wglu_ref[...], preferred_element_type=F32) + bglu_ref[...]
    glu = yg * (1.0 / (1.0 + jnp.exp(-gate)))
    sn = _rms_norm(glu, gs_ref[...]).astype(BF16)
    out = jnp.dot(pout_ref[...], sn, preferred_element_type=F32).astype(BF16)
    o_ref[...] = out.reshape(o_ref.shape)


def _ssm(u3, wb, a_re, a_im, wc, d_row, wglu_bf, bglu_row, gs_row):
    bsz, seq, width = u3.shape
    assert bsz == SUBLANES
    tt = SSM_TT
    rows = tt * bsz
    n_pairs = wb.shape[0]
    r = np.arange(rows)
    perm = np.zeros((rows, rows), np.float32)
    perm[r, (r % bsz) * tt + r // bsz] = 1.0
    pin = jnp.asarray(np.concatenate([perm, perm, perm], axis=1), BF16)
    pout = jnp.asarray(perm.T, BF16)
    const = lambda shape: pl.BlockSpec(shape, lambda i: (0,) * len(shape))
    consts = (pin, pout, wb, a_re, a_im, wc, d_row, wglu_bf, bglu_row, gs_row)
    return pl.pallas_call(
        functools.partial(_ssm_kernel, tt=tt, n_pairs=n_pairs),
        grid=(seq // tt,),
        in_specs=[pl.BlockSpec((bsz, tt, width), lambda i: (0, i, 0))] + [const(c.shape) for c in consts],
        out_specs=pl.BlockSpec((bsz, tt, width), lambda i: (0, i, 0)),
        out_shape=jax.ShapeDtypeStruct((bsz, seq, width), BF16),
        scratch_shapes=[pltpu.VMEM((n_pairs, SUBLANES, LANES), F32),
                        pltpu.VMEM((n_pairs, SUBLANES, LANES), F32),
                        pltpu.VMEM((rows, width), F32),
                        pltpu.VMEM((rows, SSM_PAIRS_PER_PASS * 2 * LANES), F32),
                        pltpu.VMEM((rows, SSM_PAIRS_PER_PASS * 2 * LANES), F32),
                        pltpu.VMEM((rows, n_pairs * 2 * LANES), BF16),
                        pltpu.VMEM((rows, width), F32)],
        compiler_params=pltpu.CompilerParams(
            dimension_semantics=("arbitrary",), vmem_limit_bytes=VMEM_LIMIT),
        name="s5_glu",
    )(u3, *consts)


def _ssm_params(a_re, a_im, log_dt, b_re, b_im, c_re, c_im):
    g, n = a_re.shape
    p = b_re.shape[2]
    a_re = a_re.astype(F32)
    a_im = a_im.astype(F32)
    dt = jnp.exp(log_dt.astype(F32))[:, None]
    decay = jnp.exp(dt * a_re)
    ab_re = decay * jnp.cos(dt * a_im)
    ab_im = decay * jnp.sin(dt * a_im)
    den = a_re * a_re + a_im * a_im
    zr = ab_re - 1.0
    f_re = (zr * a_re + ab_im * a_im) / den
    f_im = (ab_im * a_re - zr * a_im) / den
    b_re = b_re.astype(F32)
    b_im = b_im.astype(F32)
    bb_re = f_re[..., None] * b_re - f_im[..., None] * b_im
    bb_im = f_re[..., None] * b_im + f_im[..., None] * b_re
    n_pairs = g // 2
    slab_pairs = LANES // (2 * p)
    eye2 = jnp.eye(2, dtype=F32)

    def pack_b(bb):
        bt = jnp.swapaxes(bb, 1, 2).reshape(n_pairs, 2, p, n)
        blk = jnp.einsum("qapn,ab->qapbn", bt, eye2).reshape(n_pairs, 2 * p, 2 * n)
        pos = jnp.asarray(np.eye(slab_pairs, dtype=np.float32)[np.arange(n_pairs) % slab_pairs])
        return jnp.einsum("qj,qrs->qjrs", pos, blk).reshape(n_pairs, LANES, 2 * n)

    wb = jnp.concatenate([pack_b(bb_re), pack_b(bb_im)], axis=2).astype(BF16)

    ow = 2 * LANES
    ppo = ow // (2 * p)
    n_out = g * p // ow
    cc = jnp.stack([c_re.astype(F32), -c_im.astype(F32)])
    ct = jnp.swapaxes(cc, 2, 3).reshape(2, n_out, ppo, 2, n, p)
    eye_q = jnp.eye(ppo, dtype=F32)
    wc = jnp.einsum("ajqgnp,qr,gh->jqagnrhp", ct, eye_q, eye2).reshape(n_out, ppo * 4 * n, ow).astype(BF16)
    bcast = lambda a: jnp.broadcast_to(a.reshape(n_pairs, 1, 2 * n), (n_pairs, SUBLANES, 2 * n))
    return wb, bcast(ab_re), bcast(ab_im), wc


def _outproj_kernel(attn_ref, sn_ref, x_ref, wa_ref, ws_ref, ga_ref, g_ref, b_ref, o_ref):
    for sb in range(x_ref.shape[0] // OUT_SUB):
        rows = slice(sb * OUT_SUB, (sb + 1) * OUT_SUB)
        an = _rms_norm(attn_ref[rows, :], ga_ref[...]).astype(BF16)
        mixed = jnp.dot(an, wa_ref[...], preferred_element_type=F32)
        mixed = mixed + jnp.dot(sn_ref[rows, :], ws_ref[...], preferred_element_type=F32)
        h = DEEPNORM_ALPHA * x_ref[rows, :] + mixed
        o_ref[rows, :] = _layer_norm(h, g_ref[...], b_ref[...])


def _out_proj(attn, sn, x2, wo, ga, g, b):
    t, d = x2.shape
    wa_rows = attn.shape[1]
    ws_rows = sn.shape[1]
    assert wa_rows == ws_rows and wa_rows + ws_rows == wo.shape[0]
    tm = OUT_TM
    row = lambda w: pl.BlockSpec((tm, w), lambda i: (i, 0))
    const = lambda shape: pl.BlockSpec(shape, lambda i: (0,) * len(shape))
    return pl.pallas_call(
        _outproj_kernel,
        grid=(t // tm,),
        in_specs=[row(wa_rows), row(ws_rows), row(d),
                  pl.BlockSpec((wa_rows, d), lambda i: (0, 0)),
                  pl.BlockSpec((ws_rows, d), lambda i: (1, 0)),
                  const(ga.shape), const(g.shape), const(b.shape)],
        out_specs=row(d),
        out_shape=jax.ShapeDtypeStruct((t, d), F32),
        compiler_params=pltpu.CompilerParams(
            dimension_semantics=("arbitrary",), vmem_limit_bytes=VMEM_LIMIT),
        name="out_proj_ln1",
    )(attn, sn, x2, wo, wo, ga, g, b)


def _ffn_kernel(x_ref, halo_ref, wg_ref, wv_ref, cw_ref, cb_ref, wo_ref, g_ref, b_ref, o_ref,
                xb_ref, p_ref, *, blocks_per_seq):
    i = pl.program_id(0)
    f = pl.program_id(1)
    tm = x_ref.shape[0]

    @pl.when(f == 0)
    def _():
        keep = (i % blocks_per_seq != 0).astype(F32)
        xb_ref[0:FFN_HALO, :] = (halo_ref[...] * keep).astype(BF16)
        x = x_ref[...]
        xb_ref[FFN_HALO:, :] = x.astype(BF16)
        o_ref[...] = DEEPNORM_ALPHA * x

    xb = xb_ref[...]
    p_ref[...] = jnp.dot(xb, wg_ref[...], preferred_element_type=F32)
    val = jnp.dot(xb[FFN_HALO:], wv_ref[...], preferred_element_type=F32)
    gate = cb_ref[...]
    for k in range(CONV_WIDTH):
        off = FFN_HALO - (CONV_WIDTH - 1) + k
        gate = gate + cw_ref[k:k + 1, :] * p_ref[off:off + tm, :]
    hidden = (_gelu_tanh(gate) * val).astype(BF16)
    o_ref[...] += jnp.dot(hidden, wo_ref[...], preferred_element_type=F32)

    @pl.when(f == pl.num_programs(1) - 1)
    def _():
        o_ref[...] = _layer_norm(o_ref[...], g_ref[...], b_ref[...])


def _ffn(x1, w_in_bf, cw, cb, wo, g, b, seq):
    t, d = x1.shape
    dff = wo.shape[0]
    tm, tf = FFN_TM, FFN_TF
    assert seq % tm == 0 and dff % tf == 0 and tm % FFN_HALO == 0 and w_in_bf.shape[1] == 2 * dff
    hb = tm // FFN_HALO
    nf = dff // tf
    const = lambda shape: pl.BlockSpec(shape, lambda i, f: (0,) * len(shape))
    return pl.pallas_call(
        functools.partial(_ffn_kernel, blocks_per_seq=seq // tm),
        grid=(t // tm, nf),
        in_specs=[pl.BlockSpec((tm, d), lambda i, f: (i, 0)),
                  pl.BlockSpec((FFN_HALO, d), lambda i, f: (jnp.maximum(i * hb - 1, 0), 0)),
                  pl.BlockSpec((d, tf), lambda i, f: (0, f)),
                  pl.BlockSpec((d, tf), lambda i, f: (0, f + nf)),
                  pl.BlockSpec((CONV_WIDTH, tf), lambda i, f: (0, f)),
                  pl.BlockSpec((1, tf), lambda i, f: (0, f)),
                  pl.BlockSpec((tf, d), lambda i, f: (f, 0)),
                  const(g.shape), const(b.shape)],
        out_specs=pl.BlockSpec((tm, d), lambda i, f: (i, 0)),
        out_shape=jax.ShapeDtypeStruct((t, d), F32),
        scratch_shapes=[pltpu.VMEM((tm + FFN_HALO, d), BF16),
                        pltpu.VMEM((tm + FFN_HALO, tf), F32)],
        compiler_params=pltpu.CompilerParams(
            dimension_semantics=("arbitrary", "arbitrary"), vmem_limit_bytes=FFN_VMEM_LIMIT),
        name="ffn_ln2",
    )(x1, x1, w_in_bf, w_in_bf, cw, cb, wo, g, b)


def kernel(x, w_in, attn_rel_bias, ssm_a_re, ssm_a_im, ssm_log_dt, ssm_b_re, ssm_b_im, ssm_c_re,
           ssm_c_im, ssm_d, w_glu, b_glu, g_attn_out, g_ssm_out, w_out, ln1_g, ln1_b, w_ffn_in,
           ffn_conv_w, ffn_conv_b, w_ffn_out, ln2_g, ln2_b):
    bsz, seq, d = x.shape
    depth = w_in.shape[0]
    n_heads = attn_rel_bias.shape[1]
    attn_w = n_heads * HEAD_DIM
    ssm_w = d - attn_w
    row = lambda a: a.reshape(1, -1).astype(F32)
    x2 = x.reshape(bsz * seq, d)
    for l in range(depth):
        col_scale = jnp.where(jnp.arange(w_in.shape[2]) < attn_w, HEAD_DIM ** -0.5 * LOG2E, 1.0)
        qkv, u = _in_proj(x2, (w_in[l] * col_scale).astype(BF16), 3 * attn_w)
        attn = _attention(qkv.reshape(bsz, seq, 3 * attn_w), _bias_table(attn_rel_bias[l]), n_heads)
        wb, a_re, a_im, wc = _ssm_params(ssm_a_re[l], ssm_a_im[l], ssm_log_dt[l], ssm_b_re[l],
                                         ssm_b_im[l], ssm_c_re[l], ssm_c_im[l])
        sn = _ssm(u.reshape(bsz, seq, ssm_w), wb, a_re, a_im, wc, row(ssm_d[l]),
                  w_glu[l].astype(BF16), row(b_glu[l]), row(g_ssm_out[l]))
        x2 = _out_proj(attn.reshape(bsz * seq, attn_w), sn.reshape(bsz * seq, ssm_w), x2,
                       w_out[l].astype(BF16), row(g_attn_out[l]), row(ln1_g[l]), row(ln1_b[l]))
        x2 = _ffn(x2, w_ffn_in[l].astype(BF16), ffn_conv_w[l].astype(F32), row(ffn_conv_b[l]),
                  w_ffn_out[l].astype(BF16), row(ln2_g[l]), row(ln2_b[l]), seq)
    return x2.reshape(bsz, seq, d)
```

```python
import functools
import math

import jax
import jax.numpy as jnp
import numpy as np
from jax import lax
from jax.experimental import pallas as pl
from jax.experimental.pallas import tpu as pltpu

F32 = jnp.float32
BF16 = jnp.bfloat16

CHUNK = 64
HEAD_DIM = 64
LEFT_CHUNKS = 8
REL_FUTURE = CHUNK - 1
REL_PAST = 128
NEG_INF = -1e30
SSM_GROUP = 16
SSM_STATE = 64
CONV_WIDTH = 3
LN_EPS = 1e-5
DEPTH = 1
DEEPNORM_ALPHA = (2.0 * DEPTH) ** 0.25
LOG2E = math.log2(math.e)

LANES = 128
SUBLANES = 8
VMEM_LIMIT = 56 * 1024 * 1024
FFN_VMEM_LIMIT = 60 * 1024 * 1024

INPROJ_TM = 1024
INPROJ_TN = 1024
ATTN_BQ = 256
ATTN_SUB = 32
ATTN_PAD = LEFT_CHUNKS * CHUNK
ATTN_PAD_BLOCKS = ATTN_PAD // ATTN_BQ
SSM_TT = 32
SSM_PAIRS_PER_PASS = 8
OUT_TM = 512
OUT_SUB = 256
FFN_TM = 1024
FFN_TF = 512
FFN_HALO = 16


def _gelu_tanh(x):
    c = math.sqrt(2.0 / math.pi)
    return 0.5 * x * (1.0 + jnp.tanh(c * (x + 0.044715 * (x * x * x))))


def _layer_norm(h, g, b):
    mu = jnp.mean(h, axis=-1, keepdims=True)
    d = h - mu
    var = jnp.mean(d * d, axis=-1, keepdims=True)
    return d * lax.rsqrt(var + LN_EPS) * g + b


def _rms_norm(h, g):
    ms = jnp.mean(h * h, axis=-1, keepdims=True)
    return h * lax.rsqrt(ms + LN_EPS) * g


def _inproj_kernel(x_ref, w_ref, qkv_ref, u_ref, xb_ref, *, n_qkv_tiles):
    j = pl.program_id(1)

    @pl.when(j == 0)
    def _():
        xb_ref[...] = x_ref[...].astype(BF16)

    @pl.when(j < n_qkv_tiles)
    def _():
        qkv_ref[...] = jnp.dot(xb_ref[...], w_ref[...], preferred_element_type=F32).astype(BF16)

    @pl.when(j >= n_qkv_tiles)
    def _():
        u_ref[...] = jnp.dot(xb_ref[...], w_ref[...], preferred_element_type=F32)


def _in_proj(x2, w_bf, n_qkv):
    t, d = x2.shape
    n = w_bf.shape[1]
    tm, tn = INPROJ_TM, INPROJ_TN
    n_u = n - n_qkv
    assert n_u == tn and n_qkv % tn == 0 and t % tm == 0
    nq_tiles = n_qkv // tn
    return pl.pallas_call(
        functools.partial(_inproj_kernel, n_qkv_tiles=nq_tiles),
        grid=(t // tm, n // tn),
        in_specs=[pl.BlockSpec((tm, d), lambda i, j: (i, 0)),
                  pl.BlockSpec((d, tn), lambda i, j: (0, j))],
        out_specs=[pl.BlockSpec((tm, tn), lambda i, j: (i, jnp.minimum(j, nq_tiles - 1))),
                   pl.BlockSpec((tm, tn), lambda i, j: (i, 0))],
        out_shape=[jax.ShapeDtypeStruct((t, n_qkv), BF16),
                   jax.ShapeDtypeStruct((t, n_u), F32)],
        scratch_shapes=[pltpu.VMEM((tm, d), BF16)],
        compiler_params=pltpu.CompilerParams(
            dimension_semantics=("arbitrary", "arbitrary"), vmem_limit_bytes=VMEM_LIMIT),
        name="in_proj",
    )(x2, w_bf)


def _attn_kernel(q_ref, k_ref, v_ref, gen_ref, o_ref, kp_ref, vp_ref, s0_ref, s1_ref, p0_ref, p1_ref,
                 tab_ref, *, seq):
    bq = ATTN_BQ
    win = ATTN_PAD + bq
    nblk = seq // bq
    zeros = jnp.zeros((ATTN_PAD, LANES), BF16)
    kp_ref[0:ATTN_PAD, :] = zeros
    vp_ref[0:ATTN_PAD, :] = zeros
    kp_ref[ATTN_PAD:, :] = k_ref[...]
    vp_ref[ATTN_PAD:, :] = v_ref[...]
    lane = lax.broadcasted_iota(jnp.int32, (bq, LANES), 1)
    first_head = lane < HEAD_DIM
    s_refs = (s0_ref, s1_ref)
    p_refs = (p0_ref, p1_ref)

    @pl.when(pl.program_id(1) == 0)
    def _():
        for v in range(ATTN_PAD_BLOCKS + 1):
            tab_ref[v] = gen_ref[...]
        for v in range(ATTN_PAD_BLOCKS):
            front = ATTN_PAD - v * bq
            tab_ref[v, :, :, 0:front] = jnp.full((2, bq, front), NEG_INF, F32)

    def scores(i, h):
        r0 = pl.multiple_of(i * bq, bq)
        q = q_ref[pl.ds(r0, bq), :]
        sel = first_head if h == 0 else jnp.logical_not(first_head)
        qh = jnp.where(sel, q, jnp.zeros_like(q))
        kw = kp_ref[pl.ds(r0, win), :]
        s_refs[h][...] = lax.dot_general(qh, kw, (((1,), (1,)), ((), ())), preferred_element_type=F32)

    def finish(i, h):
        r0 = pl.multiple_of(i * bq, bq)
        tsel = jnp.minimum(i, ATTN_PAD_BLOCKS)
        sums = []
        for sub in range(bq // ATTN_SUB):
            rows = slice(sub * ATTN_SUB, (sub + 1) * ATTN_SUB)
            s = s_refs[h][rows, :] + tab_ref[tsel, h, rows, :]
            m = jnp.max(s, axis=-1, keepdims=True)
            p = jnp.exp2(s - m)
            sums.append(jnp.sum(p, axis=-1, keepdims=True))
            p_refs[h][rows, :] = p.astype(BF16)
        vw = vp_ref[pl.ds(r0, win), :]
        o = jnp.dot(p_refs[h][...], vw, preferred_element_type=F32)
        o = o / jnp.concatenate(sums, axis=0)
        if h == 0:
            o_ref[pl.ds(r0, bq), :] = o
        else:
            o_ref[pl.ds(r0, bq), :] = jnp.where(first_head, o_ref[pl.ds(r0, bq), :], o)

    scores(0, 0)

    def body(i, carry):
        scores(i, 1)
        finish(i, 0)
        scores(i + 1, 0)
        finish(i, 1)
        return carry

    lax.fori_loop(0, nblk - 1, body, 0)
    scores(nblk - 1, 1)
    finish(nblk - 1, 0)
    finish(nblk - 1, 1)


def _attention(qkv3, table, n_heads):
    bsz, seq, width3 = qkv3.shape
    width = width3 // 3
    n_pairs = n_heads // 2
    assert width == n_heads * HEAD_DIM and 2 * HEAD_DIM == LANES and seq % ATTN_BQ == 0
    assert ATTN_PAD % ATTN_BQ == 0 and seq // ATTN_BQ > ATTN_PAD_BLOCKS
    win = ATTN_PAD + ATTN_BQ
    assert table.shape == (n_pairs, 2, ATTN_BQ, win)
    blk = lambda off: pl.BlockSpec((None, seq, LANES), lambda p, b: (b, 0, off + p))
    return pl.pallas_call(
        functools.partial(_attn_kernel, seq=seq),
        grid=(n_pairs, bsz),
        in_specs=[blk(0), blk(n_pairs), blk(2 * n_pairs),
                  pl.BlockSpec((None, 2, ATTN_BQ, win), lambda p, b: (p, 0, 0, 0))],
        out_specs=pl.BlockSpec((None, seq, LANES), lambda p, b: (b, 0, p)),
        out_shape=jax.ShapeDtypeStruct((bsz, seq, width), F32),
        scratch_shapes=[pltpu.VMEM((ATTN_PAD + seq, LANES), BF16),
                        pltpu.VMEM((ATTN_PAD + seq, LANES), BF16),
                        pltpu.VMEM((ATTN_BQ, win), F32),
                        pltpu.VMEM((ATTN_BQ, win), F32),
                        pltpu.VMEM((ATTN_BQ, win), BF16),
                        pltpu.VMEM((ATTN_BQ, win), BF16),
                        pltpu.VMEM((ATTN_PAD_BLOCKS + 1, 2, ATTN_BQ, win), F32)],
        compiler_params=pltpu.CompilerParams(
            dimension_semantics=("arbitrary", "arbitrary"), vmem_limit_bytes=VMEM_LIMIT),
        name="band_attention",
    )(qkv3, qkv3, qkv3, table)


def _bias_table(rel_bias):
    bq = ATTN_BQ
    win = ATTN_PAD + bq
    h = rel_bias.shape[0]
    period = win + bq
    d = np.arange(period)
    d = np.where(d < win, d, d - period)
    idx = np.clip(ATTN_PAD - d, -REL_FUTURE, REL_PAST) + REL_FUTURE
    w = rel_bias.astype(F32)[:, idx] * LOG2E
    toe = jnp.tile(w, (1, bq))[:, :bq * (period - 1)].reshape(h, bq, period - 1)[:, :, :win]
    tq = ATTN_PAD + np.arange(bq)[:, None]
    tk = np.arange(win)[None, :]
    dchunk = tq // CHUNK - tk // CHUNK
    allowed = (dchunk >= 0) & (dchunk <= LEFT_CHUNKS)
    return jnp.where(allowed[None], toe, NEG_INF).reshape(h // 2, 2, bq, win)


def _ssm_kernel(u_ref, pin_ref, pout_ref, wb_ref, are_ref, aim_ref, wc_ref, d_ref, wglu_ref, bglu_ref,
                gs_ref, o_ref, st_re, st_im, utm_ref, bu0_ref, bu1_ref, s_ref, yg_ref, *, tt, n_pairs):
    ppp = SSM_PAIRS_PER_PASS
    pw = 2 * LANES
    slab_pairs = LANES // (2 * SSM_GROUP)
    n_pass = n_pairs // ppp
    rows = tt * SUBLANES
    width = u_ref.shape[-1]
    bu_refs = (bu0_ref, bu1_ref)

    @pl.when(pl.program_id(0) == 0)
    def _():
        st_re[...] = jnp.zeros_like(st_re)
        st_im[...] = jnp.zeros_like(st_im)

    u_bm = u_ref[...].reshape(rows, width)
    h1 = u_bm.astype(BF16)
    r1 = u_bm - h1.astype(F32)
    h2 = r1.astype(BF16)
    h3 = (r1 - h2.astype(F32)).astype(BF16)
    utm_ref[...] = jnp.dot(pin_ref[...], jnp.concatenate([h1, h2, h3], axis=0),
                           preferred_element_type=F32)

    def input_drive(pg):
        for q in range(ppp):
            gp = pg * ppp + q
            c0 = (gp // slab_pairs) * LANES
            bu_refs[pg % 2][:, q * pw:(q + 1) * pw] = jnp.dot(
                utm_ref[:, c0:c0 + LANES].astype(BF16), wb_ref[gp], preferred_element_type=F32)

    def recurrence(pg):
        bu_ref = bu_refs[pg % 2]
        state = [(st_re[pg * ppp + q], st_im[pg * ppp + q]) for q in range(ppp)]
        decay = [(are_ref[pg * ppp + q], aim_ref[pg * ppp + q]) for q in range(ppp)]
        held = [None] * ppp
        for t in range(tt):
            r0 = t * SUBLANES
            for q in range(ppp):
                gp = pg * ppp + q
                s_re, s_im = state[q]
                a_re, a_im = decay[q]
                b_re = bu_ref[r0:r0 + SUBLANES, q * pw:q * pw + LANES]
                b_im = bu_ref[r0:r0 + SUBLANES, q * pw + LANES:(q + 1) * pw]
                n_re = a_re * s_re - a_im * s_im + b_re
                n_im = a_re * s_im + a_im * s_re + b_im
                state[q] = (n_re, n_im)
                if t % 2 == 0:
                    held[q] = (n_re, n_im)
                else:
                    p0 = r0 - SUBLANES
                    s_ref[p0:p0 + 2 * SUBLANES, gp * pw:gp * pw + LANES] = jnp.concatenate(
                        [held[q][0], n_re], axis=0).astype(BF16)
                    s_ref[p0:p0 + 2 * SUBLANES, gp * pw + LANES:(gp + 1) * pw] = jnp.concatenate(
                        [held[q][1], n_im], axis=0).astype(BF16)
        for q in range(ppp):
            st_re[pg * ppp + q] = state[q][0]
            st_im[pg * ppp + q] = state[q][1]

    kw = wc_ref.shape[1]
    ow = wc_ref.shape[2]
    assert kw == ppp * pw and n_pass == wc_ref.shape[0]

    def readout(j):
        y = jnp.dot(s_ref[:, j * kw:(j + 1) * kw], wc_ref[j], preferred_element_type=F32)
        y = y + d_ref[:, j * ow:(j + 1) * ow] * utm_ref[:, j * ow:(j + 1) * ow]
        yg_ref[:, j * ow:(j + 1) * ow] = _gelu_tanh(y)

    input_drive(0)
    for pg in range(n_pass):
        if pg + 1 < n_pass:
            input_drive(pg + 1)
        recurrence(pg)
        readout(pg)

    yg = yg_ref[...]
    gate = jnp.dot(yg.astype(BF16), wglu_ref[...], preferred_element_type=F32) + bglu_ref[...]
    glu = yg * (1.0 / (1.0 + jnp.exp(-gate)))
    sn = _rms_norm(glu, gs_ref[...]).astype(BF16)
    out = jnp.dot(pout_ref[...], sn, preferred_element_type=F32).astype(BF16)
    o_ref[...] = out.reshape(o_ref.shape)


def _ssm(u3, wb, a_re, a_im, wc, d_row, wglu_bf, bglu_row, gs_row):
    bsz, seq, width = u3.shape
    assert bsz == SUBLANES
    tt = SSM_TT
    rows = tt * bsz
    n_pairs = wb.shape[0]
    r = np.arange(rows)
    perm = np.zeros((rows, rows), np.float32)
    perm[r, (r % bsz) * tt + r // bsz] = 1.0
    pin = jnp.asarray(np.concatenate([perm, perm, perm], axis=1), BF16)
    pout = jnp.asarray(perm.T, BF16)
    const = lambda shape: pl.BlockSpec(shape, lambda i: (0,) * len(shape))
    consts = (pin, pout, wb, a_re, a_im, wc, d_row, wglu_bf, bglu_row, gs_row)
    return pl.pallas_call(
        functools.partial(_ssm_kernel, tt=tt, n_pairs=n_pairs),
        grid=(seq // tt,),
        in_specs=[pl.BlockSpec((bsz, tt, width), lambda i: (0, i, 0))] + [const(c.shape) for c in consts],
        out_specs=pl.BlockSpec((bsz, tt, width), lambda i: (0, i, 0)),
        out_shape=jax.ShapeDtypeStruct((bsz, seq, width), BF16),
        scratch_shapes=[pltpu.VMEM((n_pairs, SUBLANES, LANES), F32),
                        pltpu.VMEM((n_pairs, SUBLANES, LANES), F32),
                        pltpu.VMEM((rows, width), F32),
                        pltpu.VMEM((rows, SSM_PAIRS_PER_PASS * 2 * LANES), F32),
                        pltpu.VMEM((rows, SSM_PAIRS_PER_PASS * 2 * LANES), F32),
                        pltpu.VMEM((rows, n_pairs * 2 * LANES), BF16),
                        pltpu.VMEM((rows, width), F32)],
        compiler_params=pltpu.CompilerParams(
            dimension_semantics=("arbitrary",), vmem_limit_bytes=VMEM_LIMIT),
        name="s5_glu",
    )(u3, *consts)


def _ssm_params(a_re, a_im, log_dt, b_re, b_im, c_re, c_im):
    g, n = a_re.shape
    p = b_re.shape[2]
    a_re = a_re.astype(F32)
    a_im = a_im.astype(F32)
    dt = jnp.exp(log_dt.astype(F32))[:, None]
    decay = jnp.exp(dt * a_re)
    ab_re = decay * jnp.cos(dt * a_im)
    ab_im = decay * jnp.sin(dt * a_im)
    den = a_re * a_re + a_im * a_im
    zr = ab_re - 1.0
    f_re = (zr * a_re + ab_im * a_im) / den
    f_im = (ab_im * a_re - zr * a_im) / den
    b_re = b_re.astype(F32)
    b_im = b_im.astype(F32)
    bb_re = f_re[..., None] * b_re - f_im[..., None] * b_im
    bb_im = f_re[..., None] * b_im + f_im[..., None] * b_re
    n_pairs = g // 2
    slab_pairs = LANES // (2 * p)
    eye2 = jnp.eye(2, dtype=F32)

    def pack_b(bb):
        bt = jnp.swapaxes(bb, 1, 2).reshape(n_pairs, 2, p, n)
        blk = jnp.einsum("qapn,ab->qapbn", bt, eye2).reshape(n_pairs, 2 * p, 2 * n)
        pos = jnp.asarray(np.eye(slab_pairs, dtype=np.float32)[np.arange(n_pairs) % slab_pairs])
        return jnp.einsum("qj,qrs->qjrs", pos, blk).reshape(n_pairs, LANES, 2 * n)

    wb = jnp.concatenate([pack_b(bb_re), pack_b(bb_im)], axis=2).astype(BF16)

    ow = 2 * LANES
    ppo = ow // (2 * p)
    n_out = g * p // ow
    cc = jnp.stack([c_re.astype(F32), -c_im.astype(F32)])
    ct = jnp.swapaxes(cc, 2, 3).reshape(2, n_out, ppo, 2, n, p)
    eye_q = jnp.eye(ppo, dtype=F32)
    wc = jnp.einsum("ajqgnp,qr,gh->jqagnrhp", ct, eye_q, eye2).reshape(n_out, ppo * 4 * n, ow).astype(BF16)
    bcast = lambda a: jnp.broadcast_to(a.reshape(n_pairs, 1, 2 * n), (n_pairs, SUBLANES, 2 * n))
    return wb, bcast(ab_re), bcast(ab_im), wc


def _outproj_kernel(attn_ref, sn_ref, x_ref, wo_ref, ga_ref, g_ref, b_ref, o_ref):
    for sb in range(x_ref.shape[0] // OUT_SUB):
        rows = slice(sb * OUT_SUB, (sb + 1) * OUT_SUB)
        an = _rms_norm(attn_ref[rows, :], ga_ref[...]).astype(BF16)
        mixed = jnp.dot(jnp.concatenate([an, sn_ref[rows, :]], axis=1), wo_ref[...],
                        preferred_element_type=F32)
        h = DEEPNORM_ALPHA * x_ref[rows, :] + mixed
        o_ref[rows, :] = _layer_norm(h, g_ref[...], b_ref[...])


def _out_proj(attn, sn, x2, wo, ga, g, b):
    t, d = x2.shape
    wa_rows = attn.shape[1]
    ws_rows = sn.shape[1]
    assert wa_rows == ws_rows and wa_rows + ws_rows == wo.shape[0]
    tm = OUT_TM
    row = lambda w: pl.BlockSpec((tm, w), lambda i: (i, 0))
    const = lambda shape: pl.BlockSpec(shape, lambda i: (0,) * len(shape))
    return pl.pallas_call(
        _outproj_kernel,
        grid=(t // tm,),
        in_specs=[row(wa_rows), row(ws_rows), row(d), const(wo.shape),
                  const(ga.shape), const(g.shape), const(b.shape)],
        out_specs=row(d),
        out_shape=jax.ShapeDtypeStruct((t, d), F32),
        compiler_params=pltpu.CompilerParams(
            dimension_semantics=("arbitrary",), vmem_limit_bytes=VMEM_LIMIT),
        name="out_proj_ln1",
    )(attn, sn, x2, wo, ga, g, b)


def _ffn_kernel(x_ref, halo_ref, wg_ref, wv_ref, cw_ref, cb_ref, wo_ref, g_ref, b_ref, o_ref,
                xb_ref, p_ref, *, blocks_per_seq):
    i = pl.program_id(0)
    f = pl.program_id(1)
    tm = x_ref.shape[0]

    @pl.when(f == 0)
    def _():
        keep = (i % blocks_per_seq != 0).astype(F32)
        xb_ref[0:FFN_HALO, :] = (halo_ref[...] * keep).astype(BF16)
        x = x_ref[...]
        xb_ref[FFN_HALO:, :] = x.astype(BF16)
        o_ref[...] = DEEPNORM_ALPHA * x

    xb = xb_ref[...]
    p_ref[...] = jnp.dot(xb, wg_ref[...], preferred_element_type=F32)
    val = jnp.dot(xb[FFN_HALO:], wv_ref[...], preferred_element_type=F32)
    gate = cb_ref[...]
    for k in range(CONV_WIDTH):
        off = FFN_HALO - (CONV_WIDTH - 1) + k
        gate = gate + cw_ref[k:k + 1, :] * p_ref[off:off + tm, :]
    hidden = (_gelu_tanh(gate) * val).astype(BF16)
    o_ref[...] += jnp.dot(hidden, wo_ref[...], preferred_element_type=F32)

    @pl.when(f == pl.num_programs(1) - 1)
    def _():
        o_ref[...] = _layer_norm(o_ref[...], g_ref[...], b_ref[...])


def _ffn(x1, w_in_bf, cw, cb, wo, g, b, seq):
    t, d = x1.shape
    dff = wo.shape[0]
    tm, tf = FFN_TM, FFN_TF
    assert seq % tm == 0 and dff % tf == 0 and tm % FFN_HALO == 0 and w_in_bf.shape[1] == 2 * dff
    hb = tm // FFN_HALO
    nf = dff // tf
    const = lambda shape: pl.BlockSpec(shape, lambda i, f: (0,) * len(shape))
    return pl.pallas_call(
        functools.partial(_ffn_kernel, blocks_per_seq=seq // tm),
        grid=(t // tm, nf),
        in_specs=[pl.BlockSpec((tm, d), lambda i, f: (i, 0)),
                  pl.BlockSpec((FFN_HALO, d), lambda i, f: (jnp.maximum(i * hb - 1, 0), 0)),
                  pl.BlockSpec((d, tf), lambda i, f: (0, f)),
                  pl.BlockSpec((d, tf), lambda i, f: (0, f + nf)),
                  pl.BlockSpec((CONV_WIDTH, tf), lambda i, f: (0, f)),
                  pl.BlockSpec((1, tf), lambda i, f: (0, f)),
                  pl.BlockSpec((tf, d), lambda i, f: (f, 0)),
                  const(g.shape), const(b.shape)],
        out_specs=pl.BlockSpec((tm, d), lambda i, f: (i, 0)),
        out_shape=jax.ShapeDtypeStruct((t, d), F32),
        scratch_shapes=[pltpu.VMEM((tm + FFN_HALO, d), BF16),
                        pltpu.VMEM((tm + FFN_HALO, tf), F32)],
        compiler_params=pltpu.CompilerParams(
            dimension_semantics=("arbitrary", "arbitrary"), vmem_limit_bytes=FFN_VMEM_LIMIT),
        name="ffn_ln2",
    )(x1, x1, w_in_bf, w_in_bf, cw, cb, wo, g, b)


def kernel(x, w_in, attn_rel_bias, ssm_a_re, ssm_a_im, ssm_log_dt, ssm_b_re, ssm_b_im, ssm_c_re,
           ssm_c_im, ssm_d, w_glu, b_glu, g_attn_out, g_ssm_out, w_out, ln1_g, ln1_b, w_ffn_in,
           ffn_conv_w, ffn_conv_b, w_ffn_out, ln2_g, ln2_b):
    bsz, seq, d = x.shape
    depth = w_in.shape[0]
    n_heads = attn_rel_bias.shape[1]
    attn_w = n_heads * HEAD_DIM
    ssm_w = d - attn_w
    row = lambda a: a.reshape(1, -1).astype(F32)
    x2 = x.reshape(bsz * seq, d)
    for l in range(depth):
        col_scale = jnp.where(jnp.arange(w_in.shape[2]) < attn_w, HEAD_DIM ** -0.5 * LOG2E, 1.0)
        qkv, u = _in_proj(x2, (w_in[l] * col_scale).astype(BF16), 3 * attn_w)
        attn = _attention(qkv.reshape(bsz, seq, 3 * attn_w), _bias_table(attn_rel_bias[l]), n_heads)
        wb, a_re, a_im, wc = _ssm_params(ssm_a_re[l], ssm_a_im[l], ssm_log_dt[l], ssm_b_re[l],
                                         ssm_b_im[l], ssm_c_re[l], ssm_c_im[l])
        sn = _ssm(u.reshape(bsz, seq, ssm_w), wb, a_re, a_im, wc, row(ssm_d[l]),
                  w_glu[l].astype(BF16), row(b_glu[l]), row(g_ssm_out[l]))
        x2 = _out_proj(attn.reshape(bsz * seq, attn_w), sn.reshape(bsz * seq, ssm_w), x2,
                       w_out[l].astype(BF16), row(g_attn_out[l]), row(ln1_g[l]), row(ln1_b[l]))
        x2 = _ffn(x2, w_ffn_in[l].astype(BF16), ffn_conv_w[l].astype(F32), row(ffn_conv_b[l]),
                  w_ffn_out[l].astype(BF16), row(ln2_g[l]), row(ln2_b[l]), seq)
    return x2.reshape(bsz, seq, d)
```

```python
import functools
import math

import jax
import jax.numpy as jnp
import numpy as np
from jax import lax
from jax.experimental import pallas as pl
from jax.experimental.pallas import tpu as pltpu

F32 = jnp.float32
BF16 = jnp.bfloat16

CHUNK = 64
HEAD_DIM = 64
LEFT_CHUNKS = 8
REL_FUTURE = CHUNK - 1
REL_PAST = 128
NEG_INF = -1e30
SSM_GROUP = 16
SSM_STATE = 64
CONV_WIDTH = 3
LN_EPS = 1e-5
DEPTH = 1
DEEPNORM_ALPHA = (2.0 * DEPTH) ** 0.25
LOG2E = math.log2(math.e)

LANES = 128
SUBLANES = 8
VMEM_LIMIT = 56 * 1024 * 1024
FFN_VMEM_LIMIT = 60 * 1024 * 1024

INPROJ_TM = 1024
INPROJ_TN = 1024
ATTN_BQ = 256
ATTN_SUB = 32
ATTN_PAD = LEFT_CHUNKS * CHUNK
ATTN_PAD_BLOCKS = ATTN_PAD // ATTN_BQ
SSM_TT = 32
SSM_PAIRS_PER_PASS = 8
OUT_TM = 512
OUT_SUB = 256
FFN_TM = 1024
FFN_TF = 512
FFN_HALO = 16


def _gelu_tanh(x):
    c = math.sqrt(2.0 / math.pi)
    return 0.5 * x * (1.0 + jnp.tanh(c * (x + 0.044715 * (x * x * x))))


def _layer_norm(h, g, b):
    mu = jnp.mean(h, axis=-1, keepdims=True)
    d = h - mu
    var = jnp.mean(d * d, axis=-1, keepdims=True)
    return d * lax.rsqrt(var + LN_EPS) * g + b


def _rms_norm(h, g):
    ms = jnp.mean(h * h, axis=-1, keepdims=True)
    return h * lax.rsqrt(ms + LN_EPS) * g


def _inproj_kernel(x_ref, w_ref, qkv_ref, u_ref, xb_ref, *, n_qkv_tiles):
    j = pl.program_id(1)

    @pl.when(j == 0)
    def _():
        xb_ref[...] = x_ref[...].astype(BF16)

    @pl.when(j < n_qkv_tiles)
    def _():
        qkv_ref[...] = jnp.dot(xb_ref[...], w_ref[...], preferred_element_type=F32).astype(BF16)

    @pl.when(j >= n_qkv_tiles)
    def _():
        u_ref[...] = jnp.dot(xb_ref[...], w_ref[...], preferred_element_type=F32)


def _in_proj(x2, w_bf, n_qkv):
    t, d = x2.shape
    n = w_bf.shape[1]
    tm, tn = INPROJ_TM, INPROJ_TN
    n_u = n - n_qkv
    assert n_u == tn and n_qkv % tn == 0 and t % tm == 0
    nq_tiles = n_qkv // tn
    return pl.pallas_call(
        functools.partial(_inproj_kernel, n_qkv_tiles=nq_tiles),
        grid=(t // tm, n // tn),
        in_specs=[pl.BlockSpec((tm, d), lambda i, j: (i, 0)),
                  pl.BlockSpec((d, tn), lambda i, j: (0, j))],
        out_specs=[pl.BlockSpec((tm, tn), lambda i, j: (i, jnp.minimum(j, nq_tiles - 1))),
                   pl.BlockSpec((tm, tn), lambda i, j: (i, 0))],
        out_shape=[jax.ShapeDtypeStruct((t, n_qkv), BF16),
                   jax.ShapeDtypeStruct((t, n_u), F32)],
        scratch_shapes=[pltpu.VMEM((tm, d), BF16)],
        compiler_params=pltpu.CompilerParams(
            dimension_semantics=("arbitrary", "arbitrary"), vmem_limit_bytes=VMEM_LIMIT),
        name="in_proj",
    )(x2, w_bf)


def _attn_kernel(q_ref, k_ref, v_ref, gen_ref, o_ref, kp_ref, vp_ref, s0_ref, s1_ref, p0_ref, p1_ref,
                 tab_ref, *, seq):
    bq = ATTN_BQ
    win = ATTN_PAD + bq
    nblk = seq // bq
    zeros = jnp.zeros((ATTN_PAD, LANES), BF16)
    kp_ref[0:ATTN_PAD, :] = zeros
    vp_ref[0:ATTN_PAD, :] = zeros
    kp_ref[ATTN_PAD:, :] = k_ref[...]
    vp_ref[ATTN_PAD:, :] = v_ref[...]
    lane = lax.broadcasted_iota(jnp.int32, (bq, LANES), 1)
    first_head = lane < HEAD_DIM
    s_refs = (s0_ref, s1_ref)
    p_refs = (p0_ref, p1_ref)

    @pl.when(pl.program_id(1) == 0)
    def _():
        period = gen_ref.shape[-1]
        row = lax.broadcasted_iota(jnp.int32, (bq, win), 0)
        col = lax.broadcasted_iota(jnp.int32, (bq, win), 1)
        shift = CHUNK.bit_length() - 1
        dchunk = jnp.right_shift(row + ATTN_PAD, shift) - jnp.right_shift(col, shift)
        allowed = jnp.logical_and(dchunk >= 0, dchunk <= LEFT_CHUNKS)
        for h in range(2):
            vec = jnp.broadcast_to(gen_ref[h:h + 1, :], (bq, period))
            toe = pltpu.roll(vec, 0, 1, stride=1, stride_axis=0)[:, :win]
            gen = jnp.where(allowed, toe, NEG_INF)
            tab_ref[ATTN_PAD_BLOCKS, h] = gen
            for v in range(ATTN_PAD_BLOCKS):
                tab_ref[v, h] = jnp.where(col >= ATTN_PAD - v * bq, gen, NEG_INF)

    def scores(i, h):
        r0 = pl.multiple_of(i * bq, bq)
        q = q_ref[pl.ds(r0, bq), :]
        sel = first_head if h == 0 else jnp.logical_not(first_head)
        qh = jnp.where(sel, q, jnp.zeros_like(q))
        kw = kp_ref[pl.ds(r0, win), :]
        s_refs[h][...] = lax.dot_general(qh, kw, (((1,), (1,)), ((), ())), preferred_element_type=F32)

    def finish(i, h):
        r0 = pl.multiple_of(i * bq, bq)
        tsel = jnp.minimum(i, ATTN_PAD_BLOCKS)
        sums = []
        for sub in range(bq // ATTN_SUB):
            rows = slice(sub * ATTN_SUB, (sub + 1) * ATTN_SUB)
            s = s_refs[h][rows, :] + tab_ref[tsel, h, rows, :]
            m = jnp.max(s, axis=-1, keepdims=True)
            p = jnp.exp2(s - m)
            sums.append(jnp.sum(p, axis=-1, keepdims=True))
            p_refs[h][rows, :] = p.astype(BF16)
        vw = vp_ref[pl.ds(r0, win), :]
        o = jnp.dot(p_refs[h][...], vw, preferred_element_type=F32)
        o = o / jnp.concatenate(sums, axis=0)
        if h == 0:
            o_ref[pl.ds(r0, bq), :] = o
        else:
            o_ref[pl.ds(r0, bq), :] = jnp.where(first_head, o_ref[pl.ds(r0, bq), :], o)

    scores(0, 0)

    def body(i, carry):
        scores(i, 1)
        finish(i, 0)
        scores(i + 1, 0)
        finish(i, 1)
        return carry

    lax.fori_loop(0, nblk - 1, body, 0)
    scores(nblk - 1, 1)
    finish(nblk - 1, 0)
    finish(nblk - 1, 1)


def _attention(qkv3, table, n_heads):
    bsz, seq, width3 = qkv3.shape
    width = width3 // 3
    n_pairs = n_heads // 2
    assert width == n_heads * HEAD_DIM and 2 * HEAD_DIM == LANES and seq % ATTN_BQ == 0
    assert ATTN_PAD % ATTN_BQ == 0 and seq // ATTN_BQ > ATTN_PAD_BLOCKS
    win = ATTN_PAD + ATTN_BQ
    assert table.shape == (n_pairs, 2, win + ATTN_BQ) and CHUNK & (CHUNK - 1) == 0
    blk = lambda off: pl.BlockSpec((None, seq, LANES), lambda p, b: (b, 0, off + p))
    return pl.pallas_call(
        functools.partial(_attn_kernel, seq=seq),
        grid=(n_pairs, bsz),
        in_specs=[blk(0), blk(n_pairs), blk(2 * n_pairs),
                  pl.BlockSpec((None, 2, win + ATTN_BQ), lambda p, b: (p, 0, 0))],
        out_specs=pl.BlockSpec((None, seq, LANES), lambda p, b: (b, 0, p)),
        out_shape=jax.ShapeDtypeStruct((bsz, seq, width), F32),
        scratch_shapes=[pltpu.VMEM((ATTN_PAD + seq, LANES), BF16),
                        pltpu.VMEM((ATTN_PAD + seq, LANES), BF16),
                        pltpu.VMEM((ATTN_BQ, win), F32),
                        pltpu.VMEM((ATTN_BQ, win), F32),
                        pltpu.VMEM((ATTN_BQ, win), BF16),
                        pltpu.VMEM((ATTN_BQ, win), BF16),
                        pltpu.VMEM((ATTN_PAD_BLOCKS + 1, 2, ATTN_BQ, win), F32)],
        compiler_params=pltpu.CompilerParams(
            dimension_semantics=("arbitrary", "arbitrary"), vmem_limit_bytes=VMEM_LIMIT),
        name="band_attention",
    )(qkv3, qkv3, qkv3, table)


def _bias_table(rel_bias):
    win = ATTN_PAD + ATTN_BQ
    period = win + ATTN_BQ
    d = np.arange(period)
    d = np.where(d < win, d, d - period)
    idx = np.clip(ATTN_PAD - d, -REL_FUTURE, REL_PAST) + REL_FUTURE
    w = rel_bias.astype(F32)[:, idx] * LOG2E
    return w.reshape(rel_bias.shape[0] // 2, 2, period)


def _ssm_kernel(u_ref, pin_ref, pout_ref, wb_ref, are_ref, aim_ref, wc_ref, d_ref, wglu_ref, bglu_ref,
                gs_ref, o_ref, st_re, st_im, utm_ref, bu0_ref, bu1_ref, s_ref, yg_ref, *, tt, n_pairs):
    ppp = SSM_PAIRS_PER_PASS
    pw = 2 * LANES
    slab_pairs = LANES // (2 * SSM_GROUP)
    n_pass = n_pairs // ppp
    rows = tt * SUBLANES
    width = u_ref.shape[-1]
    bu_refs = (bu0_ref, bu1_ref)

    @pl.when(pl.program_id(0) == 0)
    def _():
        st_re[...] = jnp.zeros_like(st_re)
        st_im[...] = jnp.zeros_like(st_im)

    u_bm = u_ref[...].reshape(rows, width)
    h1 = u_bm.astype(BF16)
    r1 = u_bm - h1.astype(F32)
    h2 = r1.astype(BF16)
    h3 = (r1 - h2.astype(F32)).astype(BF16)
    utm_ref[...] = jnp.dot(pin_ref[...], jnp.concatenate([h1, h2, h3], axis=0),
                           preferred_element_type=F32)

    def input_drive(pg):
        for q in range(ppp):
            gp = pg * ppp + q
            c0 = (gp // slab_pairs) * LANES
            bu_refs[pg % 2][:, q * pw:(q + 1) * pw] = jnp.dot(
                utm_ref[:, c0:c0 + LANES].astype(BF16), wb_ref[gp], preferred_element_type=F32)

    def recurrence(pg):
        bu_ref = bu_refs[pg % 2]
        state = [(st_re[pg * ppp + q], st_im[pg * ppp + q]) for q in range(ppp)]
        decay = [(are_ref[pg * ppp + q], aim_ref[pg * ppp + q]) for q in range(ppp)]
        held = [None] * ppp
        for t in range(tt):
            r0 = t * SUBLANES
            for q in range(ppp):
                gp = pg * ppp + q
                s_re, s_im = state[q]
                a_re, a_im = decay[q]
                b_re = bu_ref[r0:r0 + SUBLANES, q * pw:q * pw + LANES]
                b_im = bu_ref[r0:r0 + SUBLANES, q * pw + LANES:(q + 1) * pw]
                n_re = a_re * s_re - a_im * s_im + b_re
                n_im = a_re * s_im + a_im * s_re + b_im
                state[q] = (n_re, n_im)
                if t % 2 == 0:
                    held[q] = (n_re, n_im)
                else:
                    p0 = r0 - SUBLANES
                    s_ref[p0:p0 + 2 * SUBLANES, gp * pw:gp * pw + LANES] = jnp.concatenate(
                        [held[q][0], n_re], axis=0).astype(BF16)
                    s_ref[p0:p0 + 2 * SUBLANES, gp * pw + LANES:(gp + 1) * pw] = jnp.concatenate(
                        [held[q][1], n_im], axis=0).astype(BF16)
        for q in range(ppp):
            st_re[pg * ppp + q] = state[q][0]
            st_im[pg * ppp + q] = state[q][1]

    kw = wc_ref.shape[1]
    ow = wc_ref.shape[2]
    assert kw == ppp * pw and n_pass == wc_ref.shape[0]

    def readout(j):
        y = jnp.dot(s_ref[:, j * kw:(j + 1) * kw], wc_ref[j], preferred_element_type=F32)
        y = y + d_ref[:, j * ow:(j + 1) * ow] * utm_ref[:, j * ow:(j + 1) * ow]
        yg_ref[:, j * ow:(j + 1) * ow] = _gelu_tanh(y)

    input_drive(0)
    for pg in range(n_pass):
        if pg + 1 < n_pass:
            input_drive(pg + 1)
        recurrence(pg)
        readout(pg)

    yg = yg_ref[...]
    gate = jnp.dot(yg.astype(BF16), wglu_ref[...], preferred_element_type=F32) + bglu_ref[...]
    glu = yg * (1.0 / (1.0 + jnp.exp(-gate)))
    sn = _rms_norm(glu, gs_ref[...]).astype(BF16)
    out = jnp.dot(pout_ref[...], sn, preferred_element_type=F32).astype(BF16)
    o_ref[...] = out.reshape(o_ref.shape)


def _ssm(u3, wb, a_re, a_im, wc, d_row, wglu_bf, bglu_row, gs_row):
    bsz, seq, width = u3.shape
    assert bsz == SUBLANES
    tt = SSM_TT
    rows = tt * bsz
    n_pairs = wb.shape[0]
    r = np.arange(rows)
    perm = np.zeros((rows, rows), np.float32)
    perm[r, (r % bsz) * tt + r // bsz] = 1.0
    pin = jnp.asarray(np.concatenate([perm, perm, perm], axis=1), BF16)
    pout = jnp.asarray(perm.T, BF16)
    const = lambda shape: pl.BlockSpec(shape, lambda i: (0,) * len(shape))
    consts = (pin, pout, wb, a_re, a_im, wc, d_row, wglu_bf, bglu_row, gs_row)
    return pl.pallas_call(
        functools.partial(_ssm_kernel, tt=tt, n_pairs=n_pairs),
        grid=(seq // tt,),
        in_specs=[pl.BlockSpec((bsz, tt, width), lambda i: (0, i, 0))] + [const(c.shape) for c in consts],
        out_specs=pl.BlockSpec((bsz, tt, width), lambda i: (0, i, 0)),
        out_shape=jax.ShapeDtypeStruct((bsz, seq, width), BF16),
        scratch_shapes=[pltpu.VMEM((n_pairs, SUBLANES, LANES), F32),
                        pltpu.VMEM((n_pairs, SUBLANES, LANES), F32),
                        pltpu.VMEM((rows, width), F32),
                        pltpu.VMEM((rows, SSM_PAIRS_PER_PASS * 2 * LANES), F32),
                        pltpu.VMEM((rows, SSM_PAIRS_PER_PASS * 2 * LANES), F32),
                        pltpu.VMEM((rows, n_pairs * 2 * LANES), BF16),
                        pltpu.VMEM((rows, width), F32)],
        compiler_params=pltpu.CompilerParams(
            dimension_semantics=("arbitrary",), vmem_limit_bytes=VMEM_LIMIT),
        name="s5_glu",
    )(u3, *consts)


def _ssm_params(a_re, a_im, log_dt, b_re, b_im, c_re, c_im):
    g, n = a_re.shape
    p = b_re.shape[2]
    a_re = a_re.astype(F32)
    a_im = a_im.astype(F32)
    dt = jnp.exp(log_dt.astype(F32))[:, None]
    decay = jnp.exp(dt * a_re)
    ab_re = decay * jnp.cos(dt * a_im)
    ab_im = decay * jnp.sin(dt * a_im)
    den = a_re * a_re + a_im * a_im
    zr = ab_re - 1.0
    f_re = (zr * a_re + ab_im * a_im) / den
    f_im = (ab_im * a_re - zr * a_im) / den
    b_re = b_re.astype(F32)
    b_im = b_im.astype(F32)
    bb_re = f_re[..., None] * b_re - f_im[..., None] * b_im
    bb_im = f_re[..., None] * b_im + f_im[..., None] * b_re
    n_pairs = g // 2
    slab_pairs = LANES // (2 * p)
    eye2 = jnp.eye(2, dtype=F32)

    def pack_b(bb):
        bt = jnp.swapaxes(bb, 1, 2).reshape(n_pairs, 2, p, n)
        blk = jnp.einsum("qapn,ab->qapbn", bt, eye2).reshape(n_pairs, 2 * p, 2 * n)
        pos = jnp.asarray(np.eye(slab_pairs, dtype=np.float32)[np.arange(n_pairs) % slab_pairs])
        return jnp.einsum("qj,qrs->qjrs", pos, blk).reshape(n_pairs, LANES, 2 * n)

    wb = jnp.concatenate([pack_b(bb_re), pack_b(bb_im)], axis=2).astype(BF16)

    ow = 2 * LANES
    ppo = ow // (2 * p)
    n_out = g * p // ow
    cc = jnp.stack([c_re.astype(F32), -c_im.astype(F32)])
    ct = jnp.swapaxes(cc, 2, 3).reshape(2, n_out, ppo, 2, n, p)
    eye_q = jnp.eye(ppo, dtype=F32)
    wc = jnp.einsum("ajqgnp,qr,gh->jqagnrhp", ct, eye_q, eye2).reshape(n_out, ppo * 4 * n, ow).astype(BF16)
    bcast = lambda a: jnp.broadcast_to(a.reshape(n_pairs, 1, 2 * n), (n_pairs, SUBLANES, 2 * n))
    return wb, bcast(ab_re), bcast(ab_im), wc


def _outproj_kernel(attn_ref, sn_ref, x_ref, wo_ref, ga_ref, g_ref, b_ref, o_ref):
    for sb in range(x_ref.shape[0] // OUT_SUB):
        rows = slice(sb * OUT_SUB, (sb + 1) * OUT_SUB)
        an = _rms_norm(attn_ref[rows, :], ga_ref[...]).astype(BF16)
        mixed = jnp.dot(jnp.concatenate([an, sn_ref[rows, :]], axis=1), wo_ref[...],
                        preferred_element_type=F32)
        h = DEEPNORM_ALPHA * x_ref[rows, :] + mixed
        o_ref[rows, :] = _layer_norm(h, g_ref[...], b_ref[...])


def _out_proj(attn, sn, x2, wo, ga, g, b):
    t, d = x2.shape
    wa_rows = attn.shape[1]
    ws_rows = sn.shape[1]
    assert wa_rows == ws_rows and wa_rows + ws_rows == wo.shape[0]
    tm = OUT_TM
    row = lambda w: pl.BlockSpec((tm, w), lambda i: (i, 0))
    const = lambda shape: pl.BlockSpec(shape, lambda i: (0,) * len(shape))
    return pl.pallas_call(
        _outproj_kernel,
        grid=(t // tm,),
        in_specs=[row(wa_rows), row(ws_rows), row(d), const(wo.shape),
                  const(ga.shape), const(g.shape), const(b.shape)],
        out_specs=row(d),
        out_shape=jax.ShapeDtypeStruct((t, d), F32),
        compiler_params=pltpu.CompilerParams(
            dimension_semantics=("arbitrary",), vmem_limit_bytes=VMEM_LIMIT),
        name="out_proj_ln1",
    )(attn, sn, x2, wo, ga, g, b)


def _ffn_kernel(x_ref, halo_ref, wg_ref, wv_ref, cw_ref, cb_ref, wo_ref, g_ref, b_ref, o_ref,
                xb_ref, p_ref, *, blocks_per_seq):
    i = pl.program_id(0)
    f = pl.program_id(1)
    tm = x_ref.shape[0]

    @pl.when(f == 0)
    def _():
        keep = (i % blocks_per_seq != 0).astype(F32)
        xb_ref[0:FFN_HALO, :] = (halo_ref[...] * keep).astype(BF16)
        x = x_ref[...]
        xb_ref[FFN_HALO:, :] = x.astype(BF16)
        o_ref[...] = DEEPNORM_ALPHA * x

    xb = xb_ref[...]
    p_ref[...] = jnp.dot(xb, wg_ref[...], preferred_element_type=F32)
    val = jnp.dot(xb[FFN_HALO:], wv_ref[...], preferred_element_type=F32)
    gate = cb_ref[...]
    for k in range(CONV_WIDTH):
        off = FFN_HALO - (CONV_WIDTH - 1) + k
        gate = gate + cw_ref[k:k + 1, :] * p_ref[off:off + tm, :]
    hidden = (_gelu_tanh(gate) * val).astype(BF16)
    o_ref[...] += jnp.dot(hidden, wo_ref[...], preferred_element_type=F32)

    @pl.when(f == pl.num_programs(1) - 1)
    def _():
        o_ref[...] = _layer_norm(o_ref[...], g_ref[...], b_ref[...])


def _ffn(x1, w_in_bf, cw, cb, wo, g, b, seq):
    t, d = x1.shape
    dff = wo.shape[0]
    tm, tf = FFN_TM, FFN_TF
    assert seq % tm == 0 and dff % tf == 0 and tm % FFN_HALO == 0 and w_in_bf.shape[1] == 2 * dff
    hb = tm // FFN_HALO
    nf = dff // tf
    const = lambda shape: pl.BlockSpec(shape, lambda i, f: (0,) * len(shape))
    return pl.pallas_call(
        functools.partial(_ffn_kernel, blocks_per_seq=seq // tm),
        grid=(t // tm, nf),
        in_specs=[pl.BlockSpec((tm, d), lambda i, f: (i, 0)),
                  pl.BlockSpec((FFN_HALO, d), lambda i, f: (jnp.maximum(i * hb - 1, 0), 0)),
                  pl.BlockSpec((d, tf), lambda i, f: (0, f)),
                  pl.BlockSpec((d, tf), lambda i, f: (0, f + nf)),
                  pl.BlockSpec((CONV_WIDTH, tf), lambda i, f: (0, f)),
                  pl.BlockSpec((1, tf), lambda i, f: (0, f)),
                  pl.BlockSpec((tf, d), lambda i, f: (f, 0)),
                  const(g.shape), const(b.shape)],
        out_specs=pl.BlockSpec((tm, d), lambda i, f: (i, 0)),
        out_shape=jax.ShapeDtypeStruct((t, d), F32),
        scratch_shapes=[pltpu.VMEM((tm + FFN_HALO, d), BF16),
                        pltpu.VMEM((tm + FFN_HALO, tf), F32)],
        compiler_params=pltpu.CompilerParams(
            dimension_semantics=("arbitrary", "arbitrary"), vmem_limit_bytes=FFN_VMEM_LIMIT),
        name="ffn_ln2",
    )(x1, x1, w_in_bf, w_in_bf, cw, cb, wo, g, b)


def kernel(x, w_in, attn_rel_bias, ssm_a_re, ssm_a_im, ssm_log_dt, ssm_b_re, ssm_b_im, ssm_c_re,
           ssm_c_im, ssm_d, w_glu, b_glu, g_attn_out, g_ssm_out, w_out, ln1_g, ln1_b, w_ffn_in,
           ffn_conv_w, ffn_conv_b, w_ffn_out, ln2_g, ln2_b):
    bsz, seq, d = x.shape
    depth = w_in.shape[0]
    n_heads = attn_rel_bias.shape[1]
    attn_w = n_heads * HEAD_DIM
    ssm_w = d - attn_w
    row = lambda a: a.reshape(1, -1).astype(F32)
    x2 = x.reshape(bsz * seq, d)
    for l in range(depth):
        col_scale = jnp.where(jnp.arange(w_in.shape[2]) < attn_w, HEAD_DIM ** -0.5 * LOG2E, 1.0)
        qkv, u = _in_proj(x2, (w_in[l] * col_scale).astype(BF16), 3 * attn_w)
        attn = _attention(qkv.reshape(bsz, seq, 3 * attn_w), _bias_table(attn_rel_bias[l]), n_heads)
        wb, a_re, a_im, wc = _ssm_params(ssm_a_re[l], ssm_a_im[l], ssm_log_dt[l], ssm_b_re[l],
                                         ssm_b_im[l], ssm_c_re[l], ssm_c_im[l])
        sn = _ssm(u.reshape(bsz, seq, ssm_w), wb, a_re, a_im, wc, row(ssm_d[l]),
                  w_glu[l].astype(BF16), row(b_glu[l]), row(g_ssm_out[l]))
        x2 = _out_proj(attn.reshape(bsz * seq, attn_w), sn.reshape(bsz * seq, ssm_w), x2,
                       w_out[l].astype(BF16), row(g_attn_out[l]), row(ln1_g[l]), row(ln1_b[l]))
        x2 = _ffn(x2, w_ffn_in[l].astype(BF16), ffn_conv_w[l].astype(F32), row(ffn_conv_b[l]),
                  w_ffn_out[l].astype(BF16), row(ln2_g[l]), row(ln2_b[l]), seq)
    return x2.reshape(bsz, seq, d)
```

```python
import functools
import math

import jax
import jax.numpy as jnp
import numpy as np
from jax import lax
from jax.experimental import pallas as pl
from jax.experimental.pallas import tpu as pltpu

F32 = jnp.float32
BF16 = jnp.bfloat16

CHUNK = 64
HEAD_DIM = 64
LEFT_CHUNKS = 8
REL_FUTURE = CHUNK - 1
REL_PAST = 128
NEG_INF = -1e30
SSM_GROUP = 16
SSM_STATE = 64
CONV_WIDTH = 3
LN_EPS = 1e-5
DEPTH = 1
DEEPNORM_ALPHA = (2.0 * DEPTH) ** 0.25
LOG2E = math.log2(math.e)

LANES = 128
SUBLANES = 8
VMEM_LIMIT = 56 * 1024 * 1024
FFN_VMEM_LIMIT = 60 * 1024 * 1024

INPROJ_TM = 1024
INPROJ_TN = 1024
ATTN_BQ = 256
ATTN_SUB = 32
ATTN_PAD = LEFT_CHUNKS * CHUNK
ATTN_PAD_BLOCKS = ATTN_PAD // ATTN_BQ
SSM_TT = 32
SSM_PAIRS_PER_PASS = 8
OUT_TM = 512
OUT_SUB = 256
FFN_TM = 1024
FFN_TF = 512
FFN_HALO = 16


def _gelu_tanh(x):
    c = math.sqrt(2.0 / math.pi)
    return 0.5 * x * (1.0 + jnp.tanh(c * (x + 0.044715 * (x * x * x))))


def _layer_norm(h, g, b):
    mu = jnp.mean(h, axis=-1, keepdims=True)
    d = h - mu
    var = jnp.mean(d * d, axis=-1, keepdims=True)
    return d * lax.rsqrt(var + LN_EPS) * g + b


def _rms_norm(h, g):
    ms = jnp.mean(h * h, axis=-1, keepdims=True)
    return h * lax.rsqrt(ms + LN_EPS) * g


def _inproj_kernel(x_ref, w_ref, qkv_ref, u_ref, xb_ref, *, n_qkv_tiles):
    j = pl.program_id(1)

    @pl.when(j == 0)
    def _():
        xb_ref[...] = x_ref[...].astype(BF16)

    @pl.when(j < n_qkv_tiles)
    def _():
        qkv_ref[...] = jnp.dot(xb_ref[...], w_ref[...], preferred_element_type=F32).astype(BF16)

    @pl.when(j >= n_qkv_tiles)
    def _():
        u_ref[...] = jnp.dot(xb_ref[...], w_ref[...], preferred_element_type=F32)


def _in_proj(x2, w_bf, n_qkv):
    t, d = x2.shape
    n = w_bf.shape[1]
    tm, tn = INPROJ_TM, INPROJ_TN
    n_u = n - n_qkv
    assert n_u == tn and n_qkv % tn == 0 and t % tm == 0
    nq_tiles = n_qkv // tn
    return pl.pallas_call(
        functools.partial(_inproj_kernel, n_qkv_tiles=nq_tiles),
        grid=(t // tm, n // tn),
        in_specs=[pl.BlockSpec((tm, d), lambda i, j: (i, 0)),
                  pl.BlockSpec((d, tn), lambda i, j: (0, j))],
        out_specs=[pl.BlockSpec((tm, tn), lambda i, j: (i, jnp.minimum(j, nq_tiles - 1))),
                   pl.BlockSpec((tm, tn), lambda i, j: (i, 0))],
        out_shape=[jax.ShapeDtypeStruct((t, n_qkv), BF16),
                   jax.ShapeDtypeStruct((t, n_u), F32)],
        scratch_shapes=[pltpu.VMEM((tm, d), BF16)],
        compiler_params=pltpu.CompilerParams(
            dimension_semantics=("arbitrary", "arbitrary"), vmem_limit_bytes=VMEM_LIMIT),
        name="in_proj",
    )(x2, w_bf)


def _attn_kernel(q_ref, k_ref, v_ref, gen_ref, o_ref, kp_ref, vp_ref, s0_ref, s1_ref, p0_ref, p1_ref,
                 tab_ref, *, seq):
    bq = ATTN_BQ
    win = ATTN_PAD + bq
    nblk = seq // bq
    zeros = jnp.zeros((ATTN_PAD, LANES), BF16)
    kp_ref[0:ATTN_PAD, :] = zeros
    vp_ref[0:ATTN_PAD, :] = zeros
    kp_ref[ATTN_PAD:, :] = k_ref[...]
    vp_ref[ATTN_PAD:, :] = v_ref[...]
    lane = lax.broadcasted_iota(jnp.int32, (bq, LANES), 1)
    first_head = lane < HEAD_DIM
    s_refs = (s0_ref, s1_ref)
    p_refs = (p0_ref, p1_ref)

    @pl.when(pl.program_id(1) == 0)
    def _():
        period = gen_ref.shape[-1]
        row = lax.broadcasted_iota(jnp.int32, (bq, win), 0)
        col = lax.broadcasted_iota(jnp.int32, (bq, win), 1)
        shift = CHUNK.bit_length() - 1
        dchunk = jnp.right_shift(row + ATTN_PAD, shift) - jnp.right_shift(col, shift)
        allowed = jnp.logical_and(dchunk >= 0, dchunk <= LEFT_CHUNKS)
        for h in range(2):
            vec = jnp.broadcast_to(gen_ref[h:h + 1, :], (bq, period))
            toe = pltpu.roll(vec, 0, 1, stride=1, stride_axis=0)[:, :win]
            gen = jnp.where(allowed, toe, NEG_INF)
            tab_ref[ATTN_PAD_BLOCKS, h] = gen
            for v in range(ATTN_PAD_BLOCKS):
                tab_ref[v, h] = jnp.where(col >= ATTN_PAD - v * bq, gen, NEG_INF)

    def scores(i, h):
        r0 = pl.multiple_of(i * bq, bq)
        q = q_ref[pl.ds(r0, bq), :]
        sel = first_head if h == 0 else jnp.logical_not(first_head)
        qh = jnp.where(sel, q, jnp.zeros_like(q))
        kw = kp_ref[pl.ds(r0, win), :]
        s_refs[h][...] = lax.dot_general(qh, kw, (((1,), (1,)), ((), ())), preferred_element_type=F32)

    def finish(i, h):
        r0 = pl.multiple_of(i * bq, bq)
        tsel = jnp.minimum(i, ATTN_PAD_BLOCKS)
        sums = []
        for sub in range(bq // ATTN_SUB):
            rows = slice(sub * ATTN_SUB, (sub + 1) * ATTN_SUB)
            s = s_refs[h][rows, :] + tab_ref[tsel, h, rows, :]
            m = jnp.max(s, axis=-1, keepdims=True)
            p = jnp.exp2(s - m)
            sums.append(jnp.sum(p, axis=-1, keepdims=True))
            p_refs[h][rows, :] = p.astype(BF16)
        vw = vp_ref[pl.ds(r0, win), :]
        o = jnp.dot(p_refs[h][...], vw, preferred_element_type=F32)
        o = o / jnp.concatenate(sums, axis=0)
        if h == 0:
            o_ref[pl.ds(r0, bq), :] = o
        else:
            o_ref[pl.ds(r0, bq), :] = jnp.where(first_head, o_ref[pl.ds(r0, bq), :], o)

    scores(0, 0)

    def body(i, carry):
        scores(i, 1)
        finish(i, 0)
        scores(i + 1, 0)
        finish(i, 1)
        return carry

    lax.fori_loop(0, nblk - 1, body, 0)
    scores(nblk - 1, 1)
    finish(nblk - 1, 0)
    finish(nblk - 1, 1)


def _attention(qkv3, table, n_heads):
    bsz, seq, width3 = qkv3.shape
    width = width3 // 3
    n_pairs = n_heads // 2
    assert width == n_heads * HEAD_DIM and 2 * HEAD_DIM == LANES and seq % ATTN_BQ == 0
    assert ATTN_PAD % ATTN_BQ == 0 and seq // ATTN_BQ > ATTN_PAD_BLOCKS
    win = ATTN_PAD + ATTN_BQ
    assert table.shape == (n_pairs, 2, win + ATTN_BQ) and CHUNK & (CHUNK - 1) == 0
    blk = lambda off: pl.BlockSpec((None, seq, LANES), lambda p, b: (b, 0, off + p))
    return pl.pallas_call(
        functools.partial(_attn_kernel, seq=seq),
        grid=(n_pairs, bsz),
        in_specs=[blk(0), blk(n_pairs), blk(2 * n_pairs),
                  pl.BlockSpec((None, 2, win + ATTN_BQ), lambda p, b: (p, 0, 0))],
        out_specs=pl.BlockSpec((None, seq, LANES), lambda p, b: (b, 0, p)),
        out_shape=jax.ShapeDtypeStruct((bsz, seq, width), F32),
        scratch_shapes=[pltpu.VMEM((ATTN_PAD + seq, LANES), BF16),
                        pltpu.VMEM((ATTN_PAD + seq, LANES), BF16),
                        pltpu.VMEM((ATTN_BQ, win), F32),
                        pltpu.VMEM((ATTN_BQ, win), F32),
                        pltpu.VMEM((ATTN_BQ, win), BF16),
                        pltpu.VMEM((ATTN_BQ, win), BF16),
                        pltpu.VMEM((ATTN_PAD_BLOCKS + 1, 2, ATTN_BQ, win), F32)],
        compiler_params=pltpu.CompilerParams(
            dimension_semantics=("arbitrary", "arbitrary"), vmem_limit_bytes=VMEM_LIMIT),
        name="band_attention",
    )(qkv3, qkv3, qkv3, table)


def _bias_table(rel_bias):
    win = ATTN_PAD + ATTN_BQ
    period = win + ATTN_BQ
    d = np.arange(period)
    d = np.where(d < win, d, d - period)
    idx = np.clip(ATTN_PAD - d, -REL_FUTURE, REL_PAST) + REL_FUTURE
    w = rel_bias.astype(F32)[:, idx] * LOG2E
    return w.reshape(rel_bias.shape[0] // 2, 2, period)


def _ssm_kernel(u_ref, pin_ref, pout_ref, wb_ref, are_ref, aim_ref, wc_ref, d_ref, wglu_ref, bglu_ref,
                gs_ref, o_ref, st_re, st_im, utm_ref, bu0_ref, bu1_ref, s_ref, yg_ref, *, tt, n_pairs):
    ppp = SSM_PAIRS_PER_PASS
    pw = 2 * LANES
    slab_pairs = LANES // (2 * SSM_GROUP)
    n_pass = n_pairs // ppp
    rows = tt * SUBLANES
    width = u_ref.shape[-1]
    bu_refs = (bu0_ref, bu1_ref)

    @pl.when(pl.program_id(0) == 0)
    def _():
        st_re[...] = jnp.zeros_like(st_re)
        st_im[...] = jnp.zeros_like(st_im)

    u_bm = u_ref[...].reshape(rows, width)
    h1 = u_bm.astype(BF16)
    r1 = u_bm - h1.astype(F32)
    h2 = r1.astype(BF16)
    h3 = (r1 - h2.astype(F32)).astype(BF16)
    utm_ref[...] = jnp.dot(pin_ref[...], jnp.concatenate([h1, h2, h3], axis=0),
                           preferred_element_type=F32)

    def input_drive(pg):
        for q in range(ppp):
            gp = pg * ppp + q
            c0 = (gp // slab_pairs) * LANES
            bu_refs[pg % 2][:, q * pw:(q + 1) * pw] = jnp.dot(
                utm_ref[:, c0:c0 + LANES].astype(BF16), wb_ref[gp], preferred_element_type=F32)

    def recurrence(pg):
        bu_ref = bu_refs[pg % 2]
        state = [(st_re[pg * ppp + q], st_im[pg * ppp + q]) for q in range(ppp)]
        decay = [(are_ref[pg * ppp + q], aim_ref[pg * ppp + q]) for q in range(ppp)]
        held = [None] * ppp
        for t in range(tt):
            r0 = t * SUBLANES
            for q in range(ppp):
                gp = pg * ppp + q
                s_re, s_im = state[q]
                a_re, a_im = decay[q]
                b_re = bu_ref[r0:r0 + SUBLANES, q * pw:q * pw + LANES]
                b_im = bu_ref[r0:r0 + SUBLANES, q * pw + LANES:(q + 1) * pw]
                n_re = a_re * s_re - a_im * s_im + b_re
                n_im = a_re * s_im + a_im * s_re + b_im
                state[q] = (n_re, n_im)
                if t % 2 == 0:
                    held[q] = (n_re, n_im)
                else:
                    p0 = r0 - SUBLANES
                    s_ref[p0:p0 + 2 * SUBLANES, gp * pw:gp * pw + LANES] = jnp.concatenate(
                        [held[q][0], n_re], axis=0).astype(BF16)
                    s_ref[p0:p0 + 2 * SUBLANES, gp * pw + LANES:(gp + 1) * pw] = jnp.concatenate(
                        [held[q][1], n_im], axis=0).astype(BF16)
        for q in range(ppp):
            st_re[pg * ppp + q] = state[q][0]
            st_im[pg * ppp + q] = state[q][1]

    kw = wc_ref.shape[1]
    ow = wc_ref.shape[2]
    assert kw == ppp * pw and n_pass == wc_ref.shape[0]

    def readout(j):
        y = jnp.dot(s_ref[:, j * kw:(j + 1) * kw], wc_ref[j], preferred_element_type=F32)
        y = y + d_ref[:, j * ow:(j + 1) * ow] * utm_ref[:, j * ow:(j + 1) * ow]
        yg_ref[:, j * ow:(j + 1) * ow] = _gelu_tanh(y)

    input_drive(0)
    for pg in range(n_pass):
        if pg + 1 < n_pass:
            input_drive(pg + 1)
        recurrence(pg)
        readout(pg)

    yg = yg_ref[...]
    gate = jnp.dot(yg.astype(BF16), wglu_ref[...], preferred_element_type=F32) + bglu_ref[...]
    glu = yg * (1.0 / (1.0 + jnp.exp(-gate)))
    sn = _rms_norm(glu, gs_ref[...]).astype(BF16)
    out = jnp.dot(pout_ref[...], sn, preferred_element_type=F32).astype(BF16)
    o_ref[...] = out.reshape(o_ref.shape)


def _ssm(u3, wb, a_re, a_im, wc, d_row, wglu_bf, bglu_row, gs_row):
    bsz, seq, width = u3.shape
    assert bsz == SUBLANES
    tt = SSM_TT
    rows = tt * bsz
    n_pairs = wb.shape[0]
    r = np.arange(rows)
    perm = np.zeros((rows, rows), np.float32)
    perm[r, (r % bsz) * tt + r // bsz] = 1.0
    pin = jnp.asarray(np.concatenate([perm, perm, perm], axis=1), BF16)
    pout = jnp.asarray(perm.T, BF16)
    const = lambda shape: pl.BlockSpec(shape, lambda i: (0,) * len(shape))
    consts = (pin, pout, wb, a_re, a_im, wc, d_row, wglu_bf, bglu_row, gs_row)
    return pl.pallas_call(
        functools.partial(_ssm_kernel, tt=tt, n_pairs=n_pairs),
        grid=(seq // tt,),
        in_specs=[pl.BlockSpec((bsz, tt, width), lambda i: (0, i, 0))] + [const(c.shape) for c in consts],
        out_specs=pl.BlockSpec((bsz, tt, width), lambda i: (0, i, 0)),
        out_shape=jax.ShapeDtypeStruct((bsz, seq, width), BF16),
        scratch_shapes=[pltpu.VMEM((n_pairs, SUBLANES, LANES), F32),
                        pltpu.VMEM((n_pairs, SUBLANES, LANES), F32),
                        pltpu.VMEM((rows, width), F32),
                        pltpu.VMEM((rows, SSM_PAIRS_PER_PASS * 2 * LANES), F32),
                        pltpu.VMEM((rows, SSM_PAIRS_PER_PASS * 2 * LANES), F32),
                        pltpu.VMEM((rows, n_pairs * 2 * LANES), BF16),
                        pltpu.VMEM((rows, width), F32)],
        compiler_params=pltpu.CompilerParams(
            dimension_semantics=("arbitrary",), vmem_limit_bytes=VMEM_LIMIT),
        name="s5_glu",
    )(u3, *consts)


def _ssm_params(a_re, a_im, log_dt, b_re, b_im, c_re, c_im):
    g, n = a_re.shape
    p = b_re.shape[2]
    a_re = a_re.astype(F32)
    a_im = a_im.astype(F32)
    dt = jnp.exp(log_dt.astype(F32))[:, None]
    decay = jnp.exp(dt * a_re)
    ab_re = decay * jnp.cos(dt * a_im)
    ab_im = decay * jnp.sin(dt * a_im)
    den = a_re * a_re + a_im * a_im
    zr = ab_re - 1.0
    f_re = (zr * a_re + ab_im * a_im) / den
    f_im = (ab_im * a_re - zr * a_im) / den
    b_re = b_re.astype(F32)
    b_im = b_im.astype(F32)
    bb_re = f_re[..., None] * b_re - f_im[..., None] * b_im
    bb_im = f_re[..., None] * b_im + f_im[..., None] * b_re
    n_pairs = g // 2
    slab_pairs = LANES // (2 * p)
    eye2 = jnp.eye(2, dtype=F32)

    def pack_b(bb):
        bt = jnp.swapaxes(bb, 1, 2).reshape(n_pairs, 2, p, n)
        blk = jnp.einsum("qapn,ab->qapbn", bt, eye2).reshape(n_pairs, 2 * p, 2 * n)
        pos = jnp.asarray(np.eye(slab_pairs, dtype=np.float32)[np.arange(n_pairs) % slab_pairs])
        return jnp.einsum("qj,qrs->qjrs", pos, blk).reshape(n_pairs, LANES, 2 * n)

    wb = jnp.concatenate([pack_b(bb_re), pack_b(bb_im)], axis=2).astype(BF16)

    ow = 2 * LANES
    ppo = ow // (2 * p)
    n_out = g * p // ow
    cc = jnp.stack([c_re.astype(F32), -c_im.astype(F32)])
    ct = jnp.swapaxes(cc, 2, 3).reshape(2, n_out, ppo, 2, n, p)
    cols = np.arange(ow)
    spread = jnp.asarray((cols[None, :] % p == np.arange(p)[:, None]).astype(np.float32))
    tiled = jnp.einsum("ajqgnp,pc->jqagnc", ct, spread, precision=lax.Precision.HIGHEST)
    owner = (np.arange(ppo)[:, None] * 2 + np.arange(2)[None, :])[None, :, None, :, None, None]
    wc = jnp.where(cols // p == owner, tiled, 0.0).reshape(n_out, ppo * 4 * n, ow).astype(BF16)
    bcast = lambda a: jnp.broadcast_to(a.reshape(n_pairs, 1, 2 * n), (n_pairs, SUBLANES, 2 * n))
    return wb, bcast(ab_re), bcast(ab_im), wc


def _outproj_kernel(attn_ref, sn_ref, x_ref, wo_ref, ga_ref, g_ref, b_ref, o_ref):
    for sb in range(x_ref.shape[0] // OUT_SUB):
        rows = slice(sb * OUT_SUB, (sb + 1) * OUT_SUB)
        an = _rms_norm(attn_ref[rows, :], ga_ref[...]).astype(BF16)
        mixed = jnp.dot(jnp.concatenate([an, sn_ref[rows, :]], axis=1), wo_ref[...],
                        preferred_element_type=F32)
        h = DEEPNORM_ALPHA * x_ref[rows, :] + mixed
        o_ref[rows, :] = _layer_norm(h, g_ref[...], b_ref[...])


def _out_proj(attn, sn, x2, wo, ga, g, b):
    t, d = x2.shape
    wa_rows = attn.shape[1]
    ws_rows = sn.shape[1]
    assert wa_rows == ws_rows and wa_rows + ws_rows == wo.shape[0]
    tm = OUT_TM
    row = lambda w: pl.BlockSpec((tm, w), lambda i: (i, 0))
    const = lambda shape: pl.BlockSpec(shape, lambda i: (0,) * len(shape))
    return pl.pallas_call(
        _outproj_kernel,
        grid=(t // tm,),
        in_specs=[row(wa_rows), row(ws_rows), row(d), const(wo.shape),
                  const(ga.shape), const(g.shape), const(b.shape)],
        out_specs=row(d),
        out_shape=jax.ShapeDtypeStruct((t, d), F32),
        compiler_params=pltpu.CompilerParams(
            dimension_semantics=("arbitrary",), vmem_limit_bytes=VMEM_LIMIT),
        name="out_proj_ln1",
    )(attn, sn, x2, wo, ga, g, b)


def _ffn_kernel(x_ref, halo_ref, wg_ref, wv_ref, cw_ref, cb_ref, wo_ref, g_ref, b_ref, o_ref,
                xb_ref, p_ref, *, blocks_per_seq):
    i = pl.program_id(0)
    f = pl.program_id(1)
    tm = x_ref.shape[0]

    @pl.when(f == 0)
    def _():
        keep = (i % blocks_per_seq != 0).astype(F32)
        xb_ref[0:FFN_HALO, :] = (halo_ref[...] * keep).astype(BF16)
        x = x_ref[...]
        xb_ref[FFN_HALO:, :] = x.astype(BF16)
        o_ref[...] = DEEPNORM_ALPHA * x

    xb = xb_ref[...]
    p_ref[...] = jnp.dot(xb, wg_ref[...], preferred_element_type=F32)
    val = jnp.dot(xb[FFN_HALO:], wv_ref[...], preferred_element_type=F32)
    gate = cb_ref[...]
    for k in range(CONV_WIDTH):
        off = FFN_HALO - (CONV_WIDTH - 1) + k
        gate = gate + cw_ref[k:k + 1, :] * p_ref[off:off + tm, :]
    hidden = (_gelu_tanh(gate) * val).astype(BF16)
    o_ref[...] += jnp.dot(hidden, wo_ref[...], preferred_element_type=F32)

    @pl.when(f == pl.num_programs(1) - 1)
    def _():
        o_ref[...] = _layer_norm(o_ref[...], g_ref[...], b_ref[...])


def _ffn(x1, w_in_bf, cw, cb, wo, g, b, seq):
    t, d = x1.shape
    dff = wo.shape[0]
    tm, tf = FFN_TM, FFN_TF
    assert seq % tm == 0 and dff % tf == 0 and tm % FFN_HALO == 0 and w_in_bf.shape[1] == 2 * dff
    hb = tm // FFN_HALO
    nf = dff // tf
    const = lambda shape: pl.BlockSpec(shape, lambda i, f: (0,) * len(shape))
    return pl.pallas_call(
        functools.partial(_ffn_kernel, blocks_per_seq=seq // tm),
        grid=(t // tm, nf),
        in_specs=[pl.BlockSpec((tm, d), lambda i, f: (i, 0)),
                  pl.BlockSpec((FFN_HALO, d), lambda i, f: (jnp.maximum(i * hb - 1, 0), 0)),
                  pl.BlockSpec((d, tf), lambda i, f: (0, f)),
                  pl.BlockSpec((d, tf), lambda i, f: (0, f + nf)),
                  pl.BlockSpec((CONV_WIDTH, tf), lambda i, f: (0, f)),
                  pl.BlockSpec((1, tf), lambda i, f: (0, f)),
                  pl.BlockSpec((tf, d), lambda i, f: (f, 0)),
                  const(g.shape), const(b.shape)],
        out_specs=pl.BlockSpec((tm, d), lambda i, f: (i, 0)),
        out_shape=jax.ShapeDtypeStruct((t, d), F32),
        scratch_shapes=[pltpu.VMEM((tm + FFN_HALO, d), BF16),
                        pltpu.VMEM((tm + FFN_HALO, tf), F32)],
        compiler_params=pltpu.CompilerParams(
            dimension_semantics=("arbitrary", "arbitrary"), vmem_limit_bytes=FFN_VMEM_LIMIT),
        name="ffn_ln2",
    )(x1, x1, w_in_bf, w_in_bf, cw, cb, wo, g, b)


def kernel(x, w_in, attn_rel_bias, ssm_a_re, ssm_a_im, ssm_log_dt, ssm_b_re, ssm_b_im, ssm_c_re,
           ssm_c_im, ssm_d, w_glu, b_glu, g_attn_out, g_ssm_out, w_out, ln1_g, ln1_b, w_ffn_in,
           ffn_conv_w, ffn_conv_b, w_ffn_out, ln2_g, ln2_b):
    bsz, seq, d = x.shape
    depth = w_in.shape[0]
    n_heads = attn_rel_bias.shape[1]
    attn_w = n_heads * HEAD_DIM
    ssm_w = d - attn_w
    row = lambda a: a.reshape(1, -1).astype(F32)
    x2 = x.reshape(bsz * seq, d)
    for l in range(depth):
        col_scale = jnp.where(jnp.arange(w_in.shape[2]) < attn_w, HEAD_DIM ** -0.5 * LOG2E, 1.0)
        qkv, u = _in_proj(x2, (w_in[l] * col_scale).astype(BF16), 3 * attn_w)
        attn = _attention(qkv.reshape(bsz, seq, 3 * attn_w), _bias_table(attn_rel_bias[l]), n_heads)
        wb, a_re, a_im, wc = _ssm_params(ssm_a_re[l], ssm_a_im[l], ssm_log_dt[l], ssm_b_re[l],
                                         ssm_b_im[l], ssm_c_re[l], ssm_c_im[l])
        sn = _ssm(u.reshape(bsz, seq, ssm_w), wb, a_re, a_im, wc, row(ssm_d[l]),
                  w_glu[l].astype(BF16), row(b_glu[l]), row(g_ssm_out[l]))
        x2 = _out_proj(attn.reshape(bsz * seq, attn_w), sn.reshape(bsz * seq, ssm_w), x2,
                       w_out[l].astype(BF16), row(g_attn_out[l]), row(ln1_g[l]), row(ln1_b[l]))
        x2 = _ffn(x2, w_ffn_in[l].astype(BF16), ffn_conv_w[l].astype(F32), row(ffn_conv_b[l]),
                  w_ffn_out[l].astype(BF16), row(ln2_g[l]), row(ln2_b[l]), seq)
    return x2.reshape(bsz, seq, d)
```
